```python
import math
import jax, jax.numpy as jnp
from jax import lax
import numpy as np

D_MODEL = 2048
BATCH = 4
SEQ = 2048
DEPTH = 1
DEC_BATCH = 128
DEC_SEQ = 4
PAST_LEN = 2048
PAGE_SIZE = 128

POOL_WIDTH = D_MODEL // 2
POOL_WINDOWS = (2, 4, 8, 16)
POOL_GROUP = POOL_WIDTH // len(POOL_WINDOWS)
POOL_HIST = max(POOL_WINDOWS) - 1
N_HEADS = 16
HEAD_DIM = 64
N_KV_HEADS = 4
GROUP = N_HEADS // N_KV_HEADS
N_KV_SLOTS = 6
N_PAGED_SLOTS = 4
CMP_BLOCK = 32
SEL_BLOCK = 64
N_SELECT = 16
WINDOW = 512
FORCE_BONUS = 1.0e4
Q_BLOCK = 64
ATTN_SCALE = HEAD_DIM ** -0.5
N_BUCKETS = 32
MAX_DISTANCE = 128
D_FF = 5632
CONV_WIDTH = 3
N_BRANCHES = 2
EPS = 1e-6
Q_WIDTH = N_HEADS * HEAD_DIM
KV_WIDTH = N_KV_SLOTS * N_KV_HEADS * HEAD_DIM
SPLIT_SIZES = (POOL_WIDTH, Q_WIDTH, KV_WIDTH, 3 * N_HEADS, N_BRANCHES * D_MODEL)
IN_WIDTH = sum(SPLIT_SIZES)

kernel_name = 'hybrid_pool_nsa_convffn_decode_step'


def rmsnorm(x, g):
    xf = x.astype(jnp.float32)
    y = xf * lax.rsqrt(jnp.mean(xf * xf, axis=-1, keepdims=True) + EPS)
    return (y * g.astype(jnp.float32)).astype(x.dtype)


def rel_bucket(dist):
    n = jnp.maximum(dist, 0)
    max_exact = N_BUCKETS // 2
    nf = jnp.maximum(n, 1).astype(jnp.float32)
    large = max_exact + (jnp.log(nf / max_exact) / math.log(MAX_DISTANCE / max_exact)
                         * (N_BUCKETS - max_exact)).astype(jnp.int32)
    large = jnp.minimum(large, N_BUCKETS - 1)
    return jnp.where(n < max_exact, n, large)


def masked_softmax(s, mask):
    s = jnp.where(mask, s, -jnp.inf)
    m = jnp.max(s, axis=-1, keepdims=True)
    m = jnp.where(jnp.isfinite(m), m, 0.0)
    e = jnp.exp(s - m)
    d = jnp.sum(e, axis=-1, keepdims=True)
    return e / jnp.where(d > 0, d, 1.0)


def project_in(x, g, w_in):
    B, T, _ = x.shape
    z = rmsnorm(x, g) @ w_in
    cuts = [sum(SPLIT_SIZES[:i + 1]) for i in range(len(SPLIT_SIZES) - 1)]
    pool_in, q, kv, ng, mg = jnp.split(z, cuts, axis=-1)
    q = q.reshape(B, T, N_KV_HEADS, GROUP, HEAD_DIM)
    kv = kv.reshape(B, T, N_KV_SLOTS, N_KV_HEADS, HEAD_DIM)
    ng = ng.reshape(B, T, 3, N_KV_HEADS, GROUP)
    mg = mg.reshape(B, T, N_BRANCHES, D_MODEL)
    return pool_in, q, kv, ng, mg


def pool_mixer(x_new, hist, pos, w_grp, scale):
    T = x_new.shape[1]
    xcat = jnp.concatenate([hist, x_new], axis=1)
    xa = xcat.astype(jnp.float32)
    cs = jnp.concatenate([jnp.zeros_like(xa[:, :1]), jnp.cumsum(xa, axis=1)], axis=1)
    end = cs[:, POOL_HIST + 1:]
    outs = []
    for gi, w in enumerate(POOL_WINDOWS):
        c = slice(gi * POOL_GROUP, (gi + 1) * POOL_GROUP)
        win = end[..., c] - cs[:, POOL_HIST + 1 - w:POOL_HIST + 1 - w + T, c]
        cnt = jnp.minimum(w, pos + 1).astype(jnp.float32)[None, :, None]
        pooled = win / cnt - xa[:, POOL_HIST:, c]
        outs.append(jnp.einsum('btc,cd->btd', pooled.astype(x_new.dtype), w_grp[gi]))
    y = jnp.concatenate(outs, axis=-1) * scale
    return y, xcat[:, -POOL_HIST:]


def compress(x, pe, w1, w2):
    B, L = x.shape[:2]
    n = L // CMP_BLOCK
    blk = x[:, :n * CMP_BLOCK].reshape(B, n, CMP_BLOCK, N_KV_HEADS, HEAD_DIM) + pe[:, None, :]
    hid = jax.nn.gelu(jnp.einsum('bnlgd,lde->bnge', blk, w1), approximate=True)
    return jnp.einsum('bnge,ef->bngf', hid, w2)


def compressed_branch(q, q_pos, kc, vc, table):
    nc = kc.shape[1]
    s = jnp.einsum('btghd,bngd->bgthn', q, kc, preferred_element_type=jnp.float32) * ATTN_SCALE
    dist = q_pos[:, None] - (jnp.arange(nc) * CMP_BLOCK + CMP_BLOCK - 1)[None, :]
    bias = jnp.transpose(table[rel_bucket(dist)], (2, 0, 3, 1))[None]
    p = masked_softmax(s + bias, (dist >= 0)[None, None, :, None, :])
    o = jnp.einsum('bgthn,bngd->btghd', p.astype(vc.dtype), vc)
    return o, p


def select_blocks(p_cmp, q_pos, n_sb):
    imp = p_cmp.sum(axis=3)
    nc = imp.shape[-1]
    per = SEL_BLOCK // CMP_BLOCK
    imp = jnp.pad(imp, ((0, 0), (0, 0), (0, 0), (0, n_sb * per - nc)))
    imp = imp.reshape(imp.shape[:-1] + (n_sb, per)).sum(-1)
    blk = jnp.arange(n_sb)[None, :]
    cur = (q_pos // SEL_BLOCK)[:, None]
    forced = (blk == 0) | (blk == cur) | (blk == cur - 1)
    valid = blk * SEL_BLOCK <= q_pos[:, None]
    score = jnp.where(valid, imp + FORCE_BONUS * forced, -jnp.inf)
    _, idx = lax.top_k(score, min(N_SELECT, n_sb))
    return idx


def to_blocks(x, n_sb):
    B, L = x.shape[:2]
    x = jnp.pad(x, ((0, 0), (0, n_sb * SEL_BLOCK - L), (0, 0), (0, 0)))
    return jnp.transpose(x.reshape(B, n_sb, SEL_BLOCK, N_KV_HEADS, HEAD_DIM), (0, 3, 1, 2, 4))


def select_attend(q, q_pos, idx, ks_blk, vs_blk, table):
    B, T = q.shape[:2]
    k = idx.shape[-1]
    bi = jnp.arange(B)[:, None, None]
    gi = jnp.arange(N_KV_HEADS)[None, :, None]
    flat = idx.reshape(B, N_KV_HEADS, T * k)
    kg = ks_blk[bi, gi, flat].reshape(B, N_KV_HEADS, T, k * SEL_BLOCK, HEAD_DIM)
    vg = vs_blk[bi, gi, flat].reshape(B, N_KV_HEADS, T, k * SEL_BLOCK, HEAD_DIM)
    pos = (idx[..., None] * SEL_BLOCK + jnp.arange(SEL_BLOCK)).reshape(B, N_KV_HEADS, T, k * SEL_BLOCK)
    dist = q_pos[None, None, :, None] - pos
    s = jnp.einsum('btghd,bgtkd->bgthk', q, kg, preferred_element_type=jnp.float32) * ATTN_SCALE
    bias = table[rel_bucket(dist), gi[..., None]]
    p = masked_softmax(s + jnp.swapaxes(bias, -1, -2), (dist >= 0)[:, :, :, None, :])
    return jnp.einsum('bgthk,bgtkd->btghd', p.astype(vg.dtype), vg)


def window_attend(q, q_pos, kw, vw, k_pos, table):
    s = jnp.einsum('btghd,bkgd->bgthk', q, kw, preferred_element_type=jnp.float32) * ATTN_SCALE
    dist = q_pos[:, None] - k_pos[None, :]
    bias = jnp.transpose(table[rel_bucket(dist)], (2, 0, 3, 1))[None]
    mask = (dist >= 0) & (dist <= WINDOW) & (k_pos >= 0)[None, :]
    p = masked_softmax(s + bias, mask[None, None, :, None, :])
    return jnp.einsum('bgthk,bkgd->btghd', p.astype(vw.dtype), vw)


def nsa_core(q, q_pos, kv_full, cmp_params, table):
    pe_k, w1_k, w2_k, pe_v, w1_v, w2_v = cmp_params
    kc = compress(kv_full[:, :, 0], pe_k, w1_k, w2_k)
    vc = compress(kv_full[:, :, 1], pe_v, w1_v, w2_v)
    o_cmp, p_cmp = compressed_branch(q, q_pos, kc, vc, table)
    n_sb = -(-kv_full.shape[1] // SEL_BLOCK)
    idx = select_blocks(p_cmp, q_pos, n_sb)
    return o_cmp, idx, to_blocks(kv_full[:, :, 2], n_sb), to_blocks(kv_full[:, :, 3], n_sb)


def nsa_prompt(q, kv, cmp_params, table):
    B, T = q.shape[:2]
    q_pos = jnp.arange(T)
    o_cmp, idx, ks_blk, vs_blk = nsa_core(q, q_pos, kv[:, :, :N_PAGED_SLOTS], cmp_params, table)
    pad = ((0, 0), (WINDOW, 0), (0, 0), (0, 0))
    kw = jnp.pad(kv[:, :, 4], pad)
    vw = jnp.pad(kv[:, :, 5], pad)
    n_ch = T // Q_BLOCK
    n_top = idx.shape[-1]
    qs = jnp.moveaxis(q.reshape(B, n_ch, Q_BLOCK, N_KV_HEADS, GROUP, HEAD_DIM), 1, 0)
    ids = jnp.moveaxis(idx.reshape(B, N_KV_HEADS, n_ch, Q_BLOCK, n_top), 2, 0)

    def one_block(args):
        c, qc, ic = args
        start = c * Q_BLOCK
        qp = start + jnp.arange(Q_BLOCK)
        o_sel = select_attend(qc, qp, ic, ks_blk, vs_blk, table)
        kp = start - WINDOW + jnp.arange(Q_BLOCK + WINDOW)
        kwc = lax.dynamic_slice_in_dim(kw, start, Q_BLOCK + WINDOW, axis=1)
        vwc = lax.dynamic_slice_in_dim(vw, start, Q_BLOCK + WINDOW, axis=1)
        o_win = window_attend(qc, qp, kwc, vwc, kp, table)
        return o_sel, o_win

    o_sel, o_win = lax.map(one_block, (jnp.arange(n_ch), qs, ids))
    unblock = lambda o: jnp.moveaxis(o, 0, 1).reshape(B, T, N_KV_HEADS, GROUP, HEAD_DIM)
    return o_cmp, unblock(o_sel), unblock(o_win)


def nsa_sample(q, kv, past, win_buf, cmp_params, table):
    T = q.shape[1]
    p0 = past.shape[1]
    wb = win_buf.shape[1]
    q_pos = p0 + jnp.arange(T)
    full = jnp.concatenate([past, kv[:, :, :N_PAGED_SLOTS]], axis=1)
    o_cmp, idx, ks_blk, vs_blk = nsa_core(q, q_pos, full, cmp_params, table)
    o_sel = select_attend(q, q_pos, idx, ks_blk, vs_blk, table)
    win = jnp.concatenate([win_buf, kv[:, :, N_PAGED_SLOTS:]], axis=1)
    kp = p0 - wb + jnp.arange(wb + T)
    o_win = window_attend(q, q_pos, win[:, :, 0], win[:, :, 1], kp, table)
    return o_cmp, o_sel, o_win, win[:, -wb:]


def combine_nsa(o_cmp, o_sel, o_win, ng):
    g = jax.nn.sigmoid(ng.astype(jnp.float32)).astype(o_cmp.dtype)[..., None]
    o = g[:, :, 0] * o_cmp + g[:, :, 1] * o_sel + g[:, :, 2] * o_win
    return o.reshape(o.shape[0], o.shape[1], Q_WIDTH)


def merge_and_ffn(h, a_pool, o_nsa, mg, conv_hist, w_pool_proj, w_nsa_proj, w_out, g_post_mix,
                  g_pre_ffn, w_up, conv_w, conv_b, w_down, g_post_ffn):
    gates = jax.nn.sigmoid(mg.astype(jnp.float32)).astype(h.dtype)
    merged = gates[:, :, 0] * (a_pool @ w_pool_proj) + gates[:, :, 1] * (o_nsa @ w_nsa_proj)
    h = h + rmsnorm(merged @ w_out, g_post_mix)
    up = rmsnorm(h, g_pre_ffn) @ w_up
    val, gate = jnp.split(up, 2, axis=-1)
    T = h.shape[1]
    gcat = jnp.concatenate([conv_hist, gate], axis=1)
    conv = conv_b
    for j in range(CONV_WIDTH):
        conv = conv + conv_w[j] * gcat[:, j:j + T]
    y = (jax.nn.gelu(conv, approximate=True) * val) @ w_down
    h = h + rmsnorm(y, g_post_ffn)
    return h, gcat[:, -(CONV_WIDTH - 1):]


def setup_inputs(seed: int = 0) -> dict:
    key = jax.random.key(seed)
    k = jax.random.split(key, 32)
    f32 = jnp.float32
    n_pages = PAST_LEN // PAGE_SIZE
    n_used = DEC_BATCH * n_pages
    n_pool = n_used + max(1, n_used // 4)
    w_buf = min(WINDOW, PAST_LEN)
    nrm = lambda kk, shape, scale: scale * jax.random.normal(kk, shape, f32)
    gain = lambda kk: 1.0 + nrm(kk, (DEPTH, D_MODEL), 0.05)
    page_table = jax.random.permutation(k[0], n_pool)[:n_used].reshape(DEC_BATCH, n_pages).astype(jnp.int32)
    return {
        'x_prompt': nrm(k[1], (BATCH, SEQ, D_MODEL), 1.0),
        'x_sample': nrm(k[2], (DEC_BATCH, DEC_SEQ, D_MODEL), 1.0),
        'cache_kv': nrm(k[3], (DEPTH, n_pool, PAGE_SIZE, N_PAGED_SLOTS, N_KV_HEADS, HEAD_DIM), 1.0),
        'page_table': page_table,
        'state_kv_win': nrm(k[4], (DEPTH, DEC_BATCH, w_buf, 2, N_KV_HEADS, HEAD_DIM), 1.0),
        'state_pool': nrm(k[5], (DEPTH, DEC_BATCH, POOL_HIST, POOL_WIDTH), 1.0),
        'state_conv': nrm(k[6], (DEPTH, DEC_BATCH, CONV_WIDTH - 1, D_FF), 1.0),
        'g_pre_mix': gain(k[7]),
        'w_in': nrm(k[8], (DEPTH, D_MODEL, IN_WIDTH), D_MODEL ** -0.5),
        'pe_cmp_k': nrm(k[9], (DEPTH, CMP_BLOCK, HEAD_DIM), 0.2),
        'w1_cmp_k': nrm(k[10], (DEPTH, CMP_BLOCK, HEAD_DIM, HEAD_DIM), (CMP_BLOCK * HEAD_DIM) ** -0.5),
        'w2_cmp_k': nrm(k[11], (DEPTH, HEAD_DIM, HEAD_DIM), HEAD_DIM ** -0.5),
        'pe_cmp_v': nrm(k[12], (DEPTH, CMP_BLOCK, HEAD_DIM), 0.2),
        'w1_cmp_v': nrm(k[13], (DEPTH, CMP_BLOCK, HEAD_DIM, HEAD_DIM), (CMP_BLOCK * HEAD_DIM) ** -0.5),
        'w2_cmp_v': nrm(k[14], (DEPTH, HEAD_DIM, HEAD_DIM), HEAD_DIM ** -0.5),
        'rel_bias': nrm(k[15], (N_BUCKETS, N_HEADS), 0.5),
        'w_pool_grp': nrm(k[16], (DEPTH, len(POOL_WINDOWS), POOL_GROUP, POOL_GROUP), POOL_GROUP ** -0.5),
        'pool_scale': 1.0 + nrm(k[17], (DEPTH, POOL_WIDTH), 0.1),
        'w_pool_proj': nrm(k[18], (DEPTH, POOL_WIDTH, D_MODEL), POOL_WIDTH ** -0.5),
        'w_nsa_proj': nrm(k[19], (DEPTH, Q_WIDTH, D_MODEL), Q_WIDTH ** -0.5),
        'w_out': nrm(k[20], (DEPTH, D_MODEL, D_MODEL), D_MODEL ** -0.5),
        'g_post_mix': gain(k[21]),
        'g_pre_ffn': gain(k[22]),
        'w_up': nrm(k[23], (DEPTH, D_MODEL, 2 * D_FF), D_MODEL ** -0.5),
        'conv_w': nrm(k[24], (DEPTH, CONV_WIDTH, D_FF), CONV_WIDTH ** -0.5),
        'conv_b': nrm(k[25], (DEPTH, D_FF), 0.02),
        'w_down': nrm(k[26], (DEPTH, D_FF, D_MODEL), D_FF ** -0.5),
        'g_post_ffn': gain(k[27]),
    }


def reference(x_prompt, x_sample, cache_kv, page_table, state_kv_win, state_pool, state_conv,
              g_pre_mix, w_in, pe_cmp_k, w1_cmp_k, w2_cmp_k, pe_cmp_v, w1_cmp_v, w2_cmp_v,
              rel_bias, w_pool_grp, pool_scale, w_pool_proj, w_nsa_proj, w_out, g_post_mix,
              g_pre_ffn, w_up, conv_w, conv_b, w_down, g_post_ffn):
    table = rel_bias.reshape(N_BUCKETS, N_KV_HEADS, GROUP)
    B, T = x_prompt.shape[:2]
    Bs, Ts = x_sample.shape[:2]
    past_len = page_table.shape[1] * PAGE_SIZE
    pos_p = jnp.arange(T)
    pos_s = past_len + jnp.arange(Ts)
    hp, hs = x_prompt, x_sample
    kv_p_l, kv_s_l, win_p_l, win_s_l, pool_p_l, pool_s_l, conv_p_l, conv_s_l = ([] for _ in range(8))
    for l in range(DEPTH):
        cmp_params = (pe_cmp_k[l], w1_cmp_k[l], w2_cmp_k[l], pe_cmp_v[l], w1_cmp_v[l], w2_cmp_v[l])
        lw = (w_pool_proj[l], w_nsa_proj[l], w_out[l], g_post_mix[l], g_pre_ffn[l], w_up[l],
              conv_w[l], conv_b[l], w_down[l], g_post_ffn[l])
        pin, q, kv, ng, mg = project_in(hp, g_pre_mix[l], w_in[l])
        a, pool_new = pool_mixer(pin, jnp.zeros((B, POOL_HIST, POOL_WIDTH), pin.dtype), pos_p,
                                 w_pool_grp[l], pool_scale[l])
        o_cmp, o_sel, o_win = nsa_prompt(q, kv, cmp_params, table)
        o = combine_nsa(o_cmp, o_sel, o_win, ng)
        hp, conv_new = merge_and_ffn(hp, a, o, mg, jnp.zeros((B, CONV_WIDTH - 1, D_FF), hp.dtype), *lw)
        kv_p_l.append(kv[:, :, :N_PAGED_SLOTS])
        win_p_l.append(kv[:, T - min(WINDOW, T):, N_PAGED_SLOTS:])
        pool_p_l.append(pool_new)
        conv_p_l.append(conv_new)
        pin, q, kv, ng, mg = project_in(hs, g_pre_mix[l], w_in[l])
        a, pool_new = pool_mixer(pin, state_pool[l], pos_s, w_pool_grp[l], pool_scale[l])
        past = cache_kv[l, page_table].reshape(Bs, past_len, N_PAGED_SLOTS, N_KV_HEADS, HEAD_DIM)
        o_cmp, o_sel, o_win, win_new = nsa_sample(q, kv, past, state_kv_win[l], cmp_params, table)
        o = combine_nsa(o_cmp, o_sel, o_win, ng)
        hs, conv_new = merge_and_ffn(hs, a, o, mg, state_conv[l], *lw)
        kv_s_l.append(kv[:, :, :N_PAGED_SLOTS])
        win_s_l.append(win_new)
        pool_s_l.append(pool_new)
        conv_s_l.append(conv_new)
    return (hp, hs, jnp.stack(kv_p_l), jnp.stack(kv_s_l), jnp.stack(win_p_l), jnp.stack(win_s_l),
            jnp.stack(pool_p_l), jnp.stack(pool_s_l), jnp.stack(conv_p_l), jnp.stack(conv_s_l))
```

```python
import functools
import math

import jax
import jax.numpy as jnp
from jax import lax
from jax.experimental import pallas as pl
from jax.experimental.pallas import tpu as pltpu

F32 = jnp.float32
BF16 = jnp.bfloat16

EPS = 1e-6
POOL_WINDOWS = (2, 4, 8, 16)
POOL_HIST = max(POOL_WINDOWS) - 1
N_HEADS = 16
HEAD_DIM = 64
N_KV_HEADS = 4
GROUP = N_HEADS // N_KV_HEADS
KVD = N_KV_HEADS * HEAD_DIM
N_PAGED_SLOTS = 4
CMP_BLOCK = 32
SEL_BLOCK = 64
N_SELECT = 16
WINDOW = 512
FORCE_BONUS = 1.0e4
ATTN_SCALE = HEAD_DIM ** -0.5
N_BUCKETS = 32
MAX_DISTANCE = 128
CONV_WIDTH = 3
NEG = -1e30
LANES = 128
TQ = 256
CMP_GROUP = 4

_NT = (((1,), (1,)), ((), ()))


def _cparams(sem, vmem_mb):
    return pltpu.CompilerParams(dimension_semantics=sem, vmem_limit_bytes=vmem_mb * 1024 * 1024)


def _rms(x, g):
    ms = jnp.mean(x * x, axis=-1, keepdims=True)
    return x * lax.rsqrt(ms + EPS) * g


def _dot(a, b):
    return jnp.dot(a, b, preferred_element_type=F32)


def _dot_nt(a, b):
    return lax.dot_general(a, b, _NT, preferred_element_type=F32)


def _bucket(dist):
    n = jnp.maximum(dist, 0)
    max_exact = N_BUCKETS // 2
    nf = jnp.maximum(n, 1).astype(F32)
    large = max_exact + (jnp.log(nf / max_exact) / math.log(MAX_DISTANCE / max_exact)
                         * (N_BUCKETS - max_exact)).astype(jnp.int32)
    large = jnp.minimum(large, N_BUCKETS - 1)
    return jnp.where(n < max_exact, n, large)


def _softmax_parts(s, mask):
    s = jnp.where(mask, s, NEG)
    m = jnp.max(s, axis=-1, keepdims=True)
    e = jnp.where(mask, jnp.exp(s - m), 0.0)
    d = jnp.sum(e, axis=-1, keepdims=True)
    return e, d


def _topk_mask(score, n_blocks, k):
    lane = lax.broadcasted_iota(jnp.int32, score.shape, 1)
    rank = jnp.zeros(score.shape, F32)
    for i in range(n_blocks):
        ci = score[:, i:i + 1]
        rank = rank + jnp.where(lane > i, jnp.where(ci >= score, 1.0, 0.0), jnp.where(ci > score, 1.0, 0.0))
    return jnp.where((rank < k) & (lane < n_blocks), 1.0, 0.0)


def _block_scores(imp, q_pos, n_sb):
    blk = lax.broadcasted_iota(jnp.int32, imp.shape, 1)
    cur = q_pos // SEL_BLOCK
    forced = (blk == 0) | (blk == cur) | (blk == cur - 1)
    valid = (blk * SEL_BLOCK <= q_pos) & (blk < n_sb)
    return jnp.where(valid, imp + FORCE_BONUS * forced.astype(F32), -jnp.inf)


def _rms_matmul_kernel(x_ref, g_ref, w_ref, o_ref, xn_ref, *, act):
    @pl.when(pl.program_id(1) == 0)
    def _():
        xn_ref[...] = _rms(x_ref[...], g_ref[...]).astype(BF16)

    acc = _dot(xn_ref[...], w_ref[...])
    if act == "sigmoid":
        acc = jax.nn.sigmoid(acc)
    o_ref[...] = acc.astype(o_ref.dtype)


def rms_matmul(x, g, w, *, tm, tn, act=None, name):
    n, k = x.shape
    m = w.shape[1]
    return pl.pallas_call(
        functools.partial(_rms_matmul_kernel, act=act),
        grid=(n // tm, m // tn),
        in_specs=[pl.BlockSpec((tm, k), lambda i, j: (i, 0)),
                  pl.BlockSpec((1, k), lambda i, j: (0, 0)),
                  pl.BlockSpec((k, tn), lambda i, j: (0, j))],
        out_specs=pl.BlockSpec((tm, tn), lambda i, j: (i, j)),
        out_shape=jax.ShapeDtypeStruct((n, m), F32),
        scratch_shapes=[pltpu.VMEM((tm, k), BF16)],
        compiler_params=_cparams(("arbitrary", "arbitrary"), 48),
        name=name,
    )(x, g, w)


def _pool_prompt_kernel(x_ref, h_ref, w_ref, sc_ref, o_ref, buf_ref, *, tt):
    i = pl.program_id(1)
    pg = w_ref.shape[1]
    buf_ref[0:16, :] = jnp.where(i == 0, 0.0, h_ref[...])
    buf_ref[16:, :] = x_ref[...]
    pos = i * tt + lax.broadcasted_iota(jnp.int32, (tt, 1), 0)
    for gi, w in enumerate(POOL_WINDOWS):
        c = slice(gi * pg, (gi + 1) * pg)
        x = buf_ref[16:, c]
        win = x
        for k in range(1, w):
            win = win + buf_ref[16 - k:16 - k + tt, c]
        cnt = jnp.minimum(w, pos + 1).astype(F32)
        pooled = win / cnt - x
        o_ref[:, c] = (_dot(pooled.astype(BF16), w_ref[gi]) * sc_ref[:, c]).astype(o_ref.dtype)


def pool_prompt(z, w_grp, scale, *, batch, seq, tt):
    pw = w_grp.shape[0] * w_grp.shape[1]
    nt = seq // tt
    return pl.pallas_call(
        functools.partial(_pool_prompt_kernel, tt=tt),
        grid=(batch, nt),
        in_specs=[pl.BlockSpec((tt, pw), lambda b, i: (b * nt + i, 0)),
                  pl.BlockSpec((16, pw), lambda b, i: (jnp.maximum((b * nt + i) * (tt // 16) - 1, 0), 0)),
                  pl.BlockSpec(w_grp.shape, lambda b, i: (0, 0, 0)),
                  pl.BlockSpec((1, pw), lambda b, i: (0, 0))],
        out_specs=pl.BlockSpec((tt, pw), lambda b, i: (b * nt + i, 0)),
        out_shape=jax.ShapeDtypeStruct((batch * seq, pw), BF16),
        scratch_shapes=[pltpu.VMEM((tt + 16, pw), F32)],
        compiler_params=_cparams(("arbitrary", "arbitrary"), 32),
        name="pool_prompt",
    )(z, z, w_grp, scale)


def _pool_sample_kernel(x_ref, h_ref, w_ref, sc_ref, o_ref, *, ts, bs, past_len):
    pg = w_ref.shape[1]

    def row(j, c):
        if j < POOL_HIST:
            return h_ref[j, :, c]
        return x_ref[(j - POOL_HIST) * bs:(j - POOL_HIST + 1) * bs, c]

    for t in range(ts):
        for gi, w in enumerate(POOL_WINDOWS):
            c = slice(gi * pg, (gi + 1) * pg)
            x = row(POOL_HIST + t, c)
            win = x
            for k in range(1, w):
                win = win + row(POOL_HIST + t - k, c)
            cnt = float(min(w, past_len + t + 1))
            pooled = win / cnt - x
            o_ref[t * bs:(t + 1) * bs, c] = (_dot(pooled.astype(BF16), w_ref[gi]) * sc_ref[:, c]).astype(o_ref.dtype)


def pool_sample(z, hist_t, w_grp, scale, *, ts, bs, past_len):
    pw = w_grp.shape[0] * w_grp.shape[1]
    return pl.pallas_call(
        functools.partial(_pool_sample_kernel, ts=ts, bs=bs, past_len=past_len),
        grid=(1,),
        in_specs=[pl.BlockSpec((ts * bs, pw), lambda i: (0, 0)),
                  pl.BlockSpec(hist_t.shape, lambda i: (0, 0, 0)),
                  pl.BlockSpec(w_grp.shape, lambda i: (0, 0, 0)),
                  pl.BlockSpec((1, pw), lambda i: (0, 0))],
        out_specs=pl.BlockSpec((ts * bs, pw), lambda i: (0, 0)),
        out_shape=jax.ShapeDtypeStruct((ts * bs, pw), BF16),
        compiler_params=_cparams(("arbitrary",), 48),
        name="pool_sample",
    )(z, hist_t, w_grp, scale)


def _compress_tail(hid, w2_ref):
    hid = jax.nn.gelu(hid, approximate=True)
    return _dot(hid.astype(BF16), w2_ref[...])


def _compress_prompt_kernel(x_ref, perm_ref, pe_ref, w1_ref, w2_ref, o_ref, xs_ref, *, n_blk, chunk):
    n_chunks = x_ref.shape[0] // chunk
    blk_chunk = chunk // CMP_BLOCK
    for s in range(2):
        c = slice(s * KVD, (s + 1) * KVD)
        for ch in range(n_chunks):
            x = (x_ref[ch * chunk:(ch + 1) * chunk, c] + pe_ref[s]).astype(BF16)
            xs_ref[ch] = _dot(perm_ref[...], x).reshape(CMP_BLOCK, blk_chunk, KVD)
        hid = jnp.zeros((n_blk, KVD), F32)
        for l in range(CMP_BLOCK):
            hid = hid + _dot(xs_ref[:, l].reshape(n_blk, KVD).astype(BF16), w1_ref[s, l])
        o_ref[0, s] = _compress_tail(hid, w2_ref.at[s])


def compress_prompt(z, col_block, perm, pe_n, w1bd, w2bd, *, batch, seq):
    n_blk = seq // CMP_BLOCK
    chunk = perm.shape[0]
    assert seq % chunk == 0
    return pl.pallas_call(
        functools.partial(_compress_prompt_kernel, n_blk=n_blk, chunk=chunk),
        grid=(batch,),
        in_specs=[pl.BlockSpec((seq, 2 * KVD), lambda b: (b, col_block)),
                  pl.BlockSpec(perm.shape, lambda b: (0, 0)),
                  pl.BlockSpec(pe_n.shape, lambda b: (0, 0, 0)),
                  pl.BlockSpec(w1bd.shape, lambda b: (0, 0, 0, 0), pipeline_mode=pl.Buffered(1)),
                  pl.BlockSpec(w2bd.shape, lambda b: (0, 0, 0))],
        out_specs=pl.BlockSpec((1, 2, n_blk, KVD), lambda b: (b, 0, 0, 0)),
        out_shape=jax.ShapeDtypeStruct((batch, 2, n_blk, KVD), F32),
        scratch_shapes=[pltpu.VMEM((seq // chunk, CMP_BLOCK, chunk // CMP_BLOCK, KVD), F32)],
        compiler_params=_cparams(("arbitrary",), 48),
        name="compress_prompt",
    )(z, perm, pe_n, w1bd, w2bd)


def _compress_sample_kernel(pt_ref, *refs, n_pages, page):
    del pt_ref
    page_refs = refs[:n_pages]
    perm_ref, pe_ref, w1_ref, w2_ref, o_ref, x_ref = refs[n_pages:]
    b = pl.program_id(0)
    bl = b % CMP_GROUP
    n_pairs = n_pages // 2
    blk_pair = 2 * page // CMP_BLOCK
    for pp in range(n_pairs):
        for s in range(2):
            r = slice(s * KVD, (s + 1) * KVD)
            xt = jnp.concatenate([page_refs[2 * pp][0, r, :], page_refs[2 * pp + 1][0, r, :]], axis=1)
            xt = (xt + pe_ref[s]).astype(BF16)
            xp = _dot_nt(perm_ref[...], xt)
            x_ref[s, bl * n_pairs + pp] = xp.reshape(CMP_BLOCK, blk_pair, KVD)

    @pl.when(bl == CMP_GROUP - 1)
    def _():
        rows = CMP_GROUP * n_pairs * blk_pair
        for s in range(2):
            hid = jnp.zeros((rows, KVD), F32)
            for l in range(CMP_BLOCK):
                xl = x_ref[s, :, l].reshape(rows, KVD)
                hid = hid + _dot(xl.astype(BF16), w1_ref[s, l])
            o_ref[:, s] = _compress_tail(hid, w2_ref.at[s]).reshape(CMP_GROUP, n_pairs * blk_pair, KVD)


def compress_sample(cache_t, pt_flat, perm, pe_t, w1bd, w2bd, *, bs, n_pages):
    page = cache_t.shape[2]
    assert page == LANES and n_pages % 2 == 0 and bs % CMP_GROUP == 0
    n_pairs = n_pages // 2
    blk_pair = 2 * page // CMP_BLOCK
    n_blk = n_pairs * blk_pair

    def page_map(b, pt, *, p):
        return (pt[b * n_pages + p], 0, 0)

    const = lambda nd: (lambda b, pt: (0,) * nd)
    grid_spec = pltpu.PrefetchScalarGridSpec(
        num_scalar_prefetch=1,
        grid=(bs,),
        in_specs=[pl.BlockSpec((1, 2 * KVD, page), functools.partial(page_map, p=p)) for p in range(n_pages)]
        + [pl.BlockSpec(perm.shape, const(2)), pl.BlockSpec(pe_t.shape, const(3)),
           pl.BlockSpec(w1bd.shape, const(4), pipeline_mode=pl.Buffered(1)), pl.BlockSpec(w2bd.shape, const(3))],
        out_specs=pl.BlockSpec((CMP_GROUP, 2, n_blk, KVD), lambda b, pt: (b // CMP_GROUP, 0, 0, 0)),
        scratch_shapes=[pltpu.VMEM((2, CMP_GROUP * n_pairs, CMP_BLOCK, blk_pair, KVD), F32)],
    )
    return pl.pallas_call(
        functools.partial(_compress_sample_kernel, n_pages=n_pages, page=page),
        grid_spec=grid_spec,
        out_shape=jax.ShapeDtypeStruct((bs, 2, n_blk, KVD), F32),
        compiler_params=_cparams(("arbitrary",), 56),
        name="compress_sample",
    )(pt_flat, *([cache_t] * n_pages), perm, pe_t, w1bd, w2bd)


def _cmp_valid(q_pos, shape, n_cmp):
    lane = lax.broadcasted_iota(jnp.int32, shape, 1)
    j = lane % 64
    n = 2 * j + lane // 64
    return (n < n_cmp) & (n * CMP_BLOCK + CMP_BLOCK - 1 <= q_pos)


def _cmp_select_prompt_kernel(q_ref, kc_ref, vc_ref, bias_ref, ocmp_ref, sel_ref, *, n_cmp, n_sb):
    qb = pl.program_id(2)
    q_pos = qb * TQ + lax.broadcasted_iota(jnp.int32, (TQ, 1), 0)
    valid = _cmp_valid(q_pos, (TQ, LANES), n_cmp)
    kc = kc_ref[0, 0]
    vc = vc_ref[0, 0]
    imp = jnp.zeros((TQ, LANES), F32)
    for g in range(GROUP):
        s = _dot_nt(q_ref[0, g], kc) * ATTN_SCALE + bias_ref[g]
        e, d = _softmax_parts(s, valid)
        p = e / jnp.where(d > 0, d, 1.0)
        ocmp_ref[0, g] = _dot(p.astype(BF16), vc)
        imp = imp + p
    imp = imp + pltpu.roll(imp, 64, axis=1)
    score = _block_scores(imp, q_pos, n_sb)
    sel_ref[0, 0] = _topk_mask(score, n_sb, min(N_SELECT, n_sb))


def cmp_select_prompt(q_h, kc_p, vc_p, bias_cmp, *, seq):
    batch = q_h.shape[0]
    n_cmp = seq // CMP_BLOCK
    n_sb = -(-seq // SEL_BLOCK)
    return pl.pallas_call(
        functools.partial(_cmp_select_prompt_kernel, n_cmp=n_cmp, n_sb=n_sb),
        grid=(batch, N_KV_HEADS, seq // TQ),
        in_specs=[pl.BlockSpec((1, GROUP, TQ, HEAD_DIM), lambda b, kv, i: (b, kv, i, 0)),
                  pl.BlockSpec((1, 1, LANES, HEAD_DIM), lambda b, kv, i: (b, kv, 0, 0)),
                  pl.BlockSpec((1, 1, LANES, HEAD_DIM), lambda b, kv, i: (b, kv, 0, 0)),
                  pl.BlockSpec((GROUP, TQ, LANES), lambda b, kv, i: (kv, i, 0))],
        out_specs=[pl.BlockSpec((1, GROUP, TQ, HEAD_DIM), lambda b, kv, i: (b, kv, i, 0)),
                   pl.BlockSpec((1, 1, TQ, LANES), lambda b, kv, i: (b, kv, i, 0))],
        out_shape=[jax.ShapeDtypeStruct((batch, N_HEADS, seq, HEAD_DIM), F32),
                   jax.ShapeDtypeStruct((batch, N_KV_HEADS, seq, LANES), F32)],
        compiler_params=_cparams(("arbitrary",) * 3, 32),
        name="cmp_select_prompt",
    )(q_h, kc_p, vc_p, bias_cmp)


def _attn_prompt_kernel(far_ref, q_ref, ks_ref, vs_ref, kw_ref, vw_ref, sel_ref, ocmp_ref, ng_ref, bt_ref, o_ref):
    h = pl.program_id(1)
    qb = pl.program_id(2)
    q = q_ref[0, 0]
    selm = sel_ref[0, 0].astype(BF16)
    far = far_ref[h]
    rc = lax.broadcasted_iota(jnp.int32, (TQ, TQ), 0) - lax.broadcasted_iota(jnp.int32, (TQ, TQ), 1)
    blk_of_col = lax.broadcasted_iota(jnp.int32, (LANES, TQ), 1) // SEL_BLOCK
    blk_row = lax.broadcasted_iota(jnp.int32, (LANES, TQ), 0)

    def make_step(k_ref, v_ref, selected):
        def step(j, carry):
            m, l, acc = carry
            start = pl.multiple_of(j * TQ, TQ)
            k = k_ref[0, 0, pl.ds(start, TQ), :]
            v = v_ref[0, 0, pl.ds(start, TQ), :]
            dist = rc + (qb - j) * TQ
            bias = jnp.where(j == qb, bt_ref[0, :, TQ:], jnp.where(j == qb - 1, bt_ref[0, :, :TQ], far))
            s = _dot_nt(q, k) * ATTN_SCALE + bias
            if selected:
                expand = (blk_row == blk_of_col + j * (TQ // SEL_BLOCK)).astype(BF16)
                mask = (_dot(selm, expand) > 0.5) & (dist >= 0)
            else:
                mask = (dist >= 0) & (dist <= WINDOW)
            s = jnp.where(mask, s, NEG)
            m_new = jnp.maximum(m, jnp.max(s, axis=-1, keepdims=True))
            alpha = jnp.exp(m - m_new)
            p = jnp.where(mask, jnp.exp(s - m_new), 0.0)
            l = alpha * l + jnp.sum(p, axis=-1, keepdims=True)
            acc = alpha * acc + _dot(p.astype(BF16), v)
            return m_new, l, acc
        return step

    init = (jnp.full((TQ, 1), NEG, F32), jnp.zeros((TQ, 1), F32), jnp.zeros((TQ, HEAD_DIM), F32))
    _, l_s, acc_s = lax.fori_loop(0, qb + 1, make_step(ks_ref, vs_ref, True), init)
    first_w = jnp.maximum(qb - WINDOW // TQ, 0)
    _, l_w, acc_w = lax.fori_loop(first_w, qb + 1, make_step(kw_ref, vw_ref, False), init)
    o_sel = acc_s / jnp.where(l_s > 0, l_s, 1.0)
    o_win = acc_w / jnp.where(l_w > 0, l_w, 1.0)
    ng = ng_ref[0, 0]
    o = ng[:, 0:1] * ocmp_ref[0, 0] + ng[:, 1:2] * o_sel + ng[:, 2:3] * o_win
    o_ref[0, 0] = o.astype(o_ref.dtype)


def attn_prompt(far, q_h, ks, vs, kw, vw, sel, ocmp, ng_h, bias_tile, *, seq):
    batch = q_h.shape[0]
    assert WINDOW % TQ == 0 and seq % TQ == 0
    kv_spec = pl.BlockSpec((1, 1, seq, HEAD_DIM), lambda b, h, i: (b, h // GROUP, 0, 0))
    tok_spec = pl.BlockSpec((1, 1, TQ, HEAD_DIM), lambda b, h, i: (b, h, i, 0))
    return pl.pallas_call(
        _attn_prompt_kernel,
        grid=(batch, N_HEADS, seq // TQ),
        in_specs=[pl.BlockSpec(memory_space=pltpu.SMEM),
                  tok_spec, kv_spec, kv_spec, kv_spec, kv_spec,
                  pl.BlockSpec((1, 1, TQ, LANES), lambda b, h, i: (b, h // GROUP, i, 0)),
                  tok_spec,
                  pl.BlockSpec((1, 1, TQ, 3), lambda b, h, i: (b, h, i, 0)),
                  pl.BlockSpec((1, TQ, 2 * TQ), lambda b, h, i: (h, 0, 0))],
        out_specs=tok_spec,
        out_shape=jax.ShapeDtypeStruct((batch, N_HEADS, seq, HEAD_DIM), BF16),
        compiler_params=_cparams(("arbitrary",) * 3, 32),
        name="attn_prompt",
    )(far, q_h, ks, vs, kw, vw, sel, ocmp, ng_h, bias_tile)


def _diag_heads(x, ts):
    rows = x.shape[0]
    kv_of_row = (lax.broadcasted_iota(jnp.int32, (rows, HEAD_DIM), 0) % (N_KV_HEADS * ts)) // ts
    out = jnp.zeros((rows, HEAD_DIM), F32)
    for kv in range(N_KV_HEADS):
        out = out + jnp.where(kv_of_row == kv, x[:, kv * HEAD_DIM:(kv + 1) * HEAD_DIM], 0.0)
    return out


def _attn_sample_kernel(pt_ref, *refs, n_pages, page, ts, past_len, wb):
    del pt_ref
    page_refs = refs[:n_pages]
    (q_ref, knew_ref, win_ref, wnew_ref, kc_ref, vc_ref, bcmp_ref, bsel_ref, bseln_ref, bwin_ref, bwinn_ref,
     expand_ref, ng_ref, o_ref, kt_ref, vt_ref) = refs[n_pages:]
    rows = GROUP * N_KV_HEADS * ts
    kvt = N_KV_HEADS * ts
    n_cmp = past_len // CMP_BLOCK
    n_sb = -(-(past_len + ts) // SEL_BLOCK)
    q = q_ref[0]
    t_of_row = lax.broadcasted_iota(jnp.int32, (rows, 1), 0) % ts
    q_pos = past_len + t_of_row

    s = _dot_nt(q, kc_ref[0]) * ATTN_SCALE + bcmp_ref[...]
    e, d = _softmax_parts(s, _cmp_valid(q_pos, (rows, LANES), n_cmp))
    p = e / jnp.where(d > 0, d, 1.0)
    o_cmp = _diag_heads(_dot(p.astype(BF16), vc_ref[0]), ts)
    imp = p[0:kvt]
    for g in range(1, GROUP):
        imp = imp + p[g * kvt:(g + 1) * kvt]
    imp = imp + pltpu.roll(imp, 64, axis=1)
    sel = _topk_mask(_block_scores(imp, q_pos[0:kvt], n_sb), n_sb, min(N_SELECT, n_sb))
    sel = jnp.concatenate([sel] * GROUP, axis=0)

    for pg in range(n_pages):
        kt_ref[:, pg * page:(pg + 1) * page] = page_refs[pg][0, 0:KVD, :].astype(BF16)
        vt_ref[:, pg * page:(pg + 1) * page] = page_refs[pg][0, KVD:2 * KVD, :].astype(BF16)
    new_col = lax.broadcasted_iota(jnp.int32, (rows, 16), 1)
    new_ok = (new_col <= t_of_row) & (new_col < ts)
    s = _dot(q, kt_ref[...]) * ATTN_SCALE + bsel_ref[...]
    mask = _dot(sel.astype(BF16), expand_ref[...]) > 0.5
    sn = _dot_nt(q, knew_ref[0, :, 0:KVD]) * ATTN_SCALE + bseln_ref[...]
    mask_n = new_ok & (sel[:, past_len // SEL_BLOCK:past_len // SEL_BLOCK + 1] > 0.5)
    o_sel = _two_part_attention(s, mask, vt_ref[...], sn, mask_n, knew_ref[0, :, KVD:2 * KVD], ts)

    s = _dot(q, win_ref[0, 0:KVD, :].astype(BF16)) * ATTN_SCALE + bwin_ref[...]
    dist = wb + t_of_row - lax.broadcasted_iota(jnp.int32, (rows, wb), 1)
    mask = (dist >= 0) & (dist <= WINDOW)
    sn = _dot_nt(q, wnew_ref[0, :, 0:KVD]) * ATTN_SCALE + bwinn_ref[...]
    o_win = _two_part_attention(s, mask, win_ref[0, KVD:2 * KVD, :].astype(BF16), sn, new_ok,
                                wnew_ref[0, :, KVD:2 * KVD], ts)

    ng = ng_ref[0]
    o_ref[0] = (ng[:, 0:1] * o_cmp + ng[:, 1:2] * o_sel + ng[:, 2:3] * o_win).astype(o_ref.dtype)


def _two_part_attention(s, mask, vt, sn, mask_n, vn, ts):
    s = jnp.where(mask, s, NEG)
    sn = jnp.where(mask_n, sn, NEG)
    m = jnp.maximum(jnp.max(s, axis=-1, keepdims=True), jnp.max(sn, axis=-1, keepdims=True))
    e = jnp.where(mask, jnp.exp(s - m), 0.0)
    en = jnp.where(mask_n, jnp.exp(sn - m), 0.0)
    d = jnp.sum(e, axis=-1, keepdims=True) + jnp.sum(en, axis=-1, keepdims=True)
    inv = jnp.where(d > 0, d, 1.0)
    o = _dot_nt((e / inv).astype(BF16), vt) + _dot((en / inv).astype(BF16), vn)
    return _diag_heads(o, ts)


def attn_sample(cache_t, pt_flat, q_bd, knew, win_t, wnew, kc_s, vc_s, bcmp, bsel, bseln, bwin, bwinn, expand, ng_s,
                *, bs, ts, n_pages, past_len):
    page = cache_t.shape[2]
    wb = win_t.shape[2]
    rows = GROUP * N_KV_HEADS * ts
    assert past_len == n_pages * page and past_len % SEL_BLOCK == 0 and ts <= 16
    assert (past_len + ts) // CMP_BLOCK == past_len // CMP_BLOCK

    def page_map(b, pt, *, p):
        return (pt[b * n_pages + p], 1, 0)

    per_b = lambda shape: pl.BlockSpec((1,) + shape, lambda b, pt: (b,) + (0,) * len(shape))
    const = lambda a: pl.BlockSpec(a.shape, lambda b, pt: (0,) * a.ndim)
    grid_spec = pltpu.PrefetchScalarGridSpec(
        num_scalar_prefetch=1,
        grid=(bs,),
        in_specs=[pl.BlockSpec((1, 2 * KVD, page), functools.partial(page_map, p=p)) for p in range(n_pages)]
        + [per_b((rows, KVD)), per_b((16, 2 * KVD)), per_b((2 * KVD, wb)), per_b((16, 2 * KVD)),
           per_b((LANES, KVD)), per_b((LANES, KVD)),
           const(bcmp), const(bsel), const(bseln), const(bwin), const(bwinn), const(expand),
           per_b((rows, 3))],
        out_specs=per_b((rows, HEAD_DIM)),
        scratch_shapes=[pltpu.VMEM((KVD, past_len), BF16), pltpu.VMEM((KVD, past_len), BF16)],
    )
    return pl.pallas_call(
        functools.partial(_attn_sample_kernel, n_pages=n_pages, page=page, ts=ts, past_len=past_len, wb=wb),
        grid_spec=grid_spec,
        out_shape=jax.ShapeDtypeStruct((bs, rows, HEAD_DIM), BF16),
        compiler_params=_cparams(("arbitrary",), 48),
        name="attn_sample",
    )(pt_flat, *([cache_t] * n_pages), q_bd, knew, win_t, wnew, kc_s, vc_s, bcmp, bsel, bseln, bwin, bwinn, expand,
      ng_s)


def _merge_kernel(a_ref, o_ref, g0_ref, g1_ref, h_ref, wp_ref, wn_ref, wo_ref, gp_ref, out_ref):
    merged = g0_ref[...] * _dot(a_ref[...], wp_ref[...]) + g1_ref[...] * _dot(o_ref[...], wn_ref[...])
    r = _dot(merged.astype(BF16), wo_ref[...])
    out_ref[...] = h_ref[...] + _rms(r, gp_ref[...])


def merge(a, o, gates, h, wp, wn, wo, gp, *, tm):
    n, d = h.shape
    pw = a.shape[1]
    qw = o.shape[1]
    once = pl.Buffered(1)
    return pl.pallas_call(
        _merge_kernel,
        grid=(n // tm,),
        in_specs=[pl.BlockSpec((tm, pw), lambda i: (i, 0)),
                  pl.BlockSpec((tm, qw), lambda i: (i, 0)),
                  pl.BlockSpec((tm, d), lambda i: (i, 0)),
                  pl.BlockSpec((tm, d), lambda i: (i, 1)),
                  pl.BlockSpec((tm, d), lambda i: (i, 0)),
                  pl.BlockSpec(wp.shape, lambda i: (0, 0), pipeline_mode=once),
                  pl.BlockSpec(wn.shape, lambda i: (0, 0), pipeline_mode=once),
                  pl.BlockSpec(wo.shape, lambda i: (0, 0), pipeline_mode=once),
                  pl.BlockSpec((1, d), lambda i: (0, 0))],
        out_specs=pl.BlockSpec((tm, d), lambda i: (i, 0)),
        out_shape=jax.ShapeDtypeStruct((n, d), F32),
        compiler_params=_cparams(("arbitrary",), 56),
        name="merge",
    )(a, o, gates, gates, h, wp, wn, wo, gp)


def _ffn_kernel(h_ref, halo_ref, gpre_ref, wv_ref, wg_ref, cw_ref, cb_ref, wd_ref, gpost_ref,
                out_ref, tail_ref, xn_ref, acc_ref, gbuf_ref, *, tm, halo, shift, tiles_per_seq, tail, from_rows):
    i = pl.program_id(0)
    f = pl.program_id(1)

    @pl.when(f == 0)
    def _():
        xn_ref[halo:, :] = _rms(h_ref[...], gpre_ref[...]).astype(BF16)
        if from_rows:
            xn_ref[0:halo, :] = _rms(halo_ref[...], gpre_ref[...]).astype(BF16)
        acc_ref[...] = jnp.zeros_like(acc_ref)

    val = _dot(xn_ref[halo:, :], wv_ref[...])
    if from_rows:
        gate_all = _dot(xn_ref[...], wg_ref[...])
        gbuf_ref[0:halo, :] = jnp.where(i % tiles_per_seq == 0, 0.0, gate_all[0:halo])
        gate = gate_all[halo:]
    else:
        gbuf_ref[0:halo, :] = halo_ref[...]
        gate = _dot(xn_ref[halo:, :], wg_ref[...])
    gbuf_ref[halo:, :] = gate
    conv = cb_ref[...]
    for j in range(CONV_WIDTH):
        back = (CONV_WIDTH - 1 - j) * shift
        conv = conv + cw_ref[j:j + 1, :] * gbuf_ref[halo - back:halo - back + tm, :]
    act = jax.nn.gelu(conv, approximate=True) * val
    acc_ref[...] += _dot(act.astype(BF16), wd_ref[...])
    tail_ref[...] = gate[tm - tail:, :]

    @pl.when(f == pl.num_programs(1) - 1)
    def _():
        out_ref[...] = h_ref[...] + _rms(acc_ref[...], gpost_ref[...])


def ffn(h, halo_src, gpre, wv, wg, cw, cb, wd, gpost, *, tm, tf, halo, shift, tiles_per_seq, tail):
    n, d = h.shape
    dff = wv.shape[1]
    nt = n // tm
    from_rows = halo_src is None
    if from_rows:
        halo_src = h
        halo_spec = pl.BlockSpec((halo, d), lambda i, f: (jnp.maximum(i * (tm // halo) - 1, 0), 0))
    else:
        assert nt == 1
        halo_spec = pl.BlockSpec((halo, tf), lambda i, f: (0, f))
    return pl.pallas_call(
        functools.partial(_ffn_kernel, tm=tm, halo=halo, shift=shift, tiles_per_seq=tiles_per_seq, tail=tail,
                          from_rows=from_rows),
        grid=(nt, dff // tf),
        in_specs=[pl.BlockSpec((tm, d), lambda i, f: (i, 0)),
                  halo_spec,
                  pl.BlockSpec((1, d), lambda i, f: (0, 0)),
                  pl.BlockSpec((d, tf), lambda i, f: (0, f)),
                  pl.BlockSpec((d, tf), lambda i, f: (0, f)),
                  pl.BlockSpec((CONV_WIDTH, tf), lambda i, f: (0, f)),
                  pl.BlockSpec((1, tf), lambda i, f: (0, f)),
                  pl.BlockSpec((tf, d), lambda i, f: (f, 0)),
                  pl.BlockSpec((1, d), lambda i, f: (0, 0))],
        out_specs=[pl.BlockSpec((tm, d), lambda i, f: (i, 0)),
                   pl.BlockSpec((tail, tf), lambda i, f: (i, f))],
        out_shape=[jax.ShapeDtypeStruct((n, d), F32),
                   jax.ShapeDtypeStruct((nt * tail, dff), F32)],
        scratch_shapes=[pltpu.VMEM((tm + halo, d), BF16), pltpu.VMEM((tm, d), F32),
                        pltpu.VMEM((tm + halo, tf), F32)],
        compiler_params=_cparams(("arbitrary", "arbitrary"), 56),
        name="ffn",
    )(h, halo_src, gpre, wv, wg, cw, cb, wd, gpost)


def _block_diag4(w):
    eye = jnp.eye(N_KV_HEADS, dtype=w.dtype)
    out = jnp.einsum("gh,...ab->...gahb", eye, w)
    return out.reshape(w.shape[:-2] + (N_KV_HEADS * w.shape[-2], N_KV_HEADS * w.shape[-1]))


def _cmp_lane_layout(x, axis):
    n = x.shape[axis]
    x = jnp.moveaxis(x, axis, 0)
    ev, od = x[0::2], x[1::2]
    pad = lambda y: jnp.pad(y, ((0, 64 - y.shape[0]),) + ((0, 0),) * (y.ndim - 1))
    return jnp.moveaxis(jnp.concatenate([pad(ev), pad(od)], axis=0), 0, axis)


def _cmp_dist(q_pos):
    lane = jnp.arange(LANES)
    n = 2 * (lane % 64) + lane // 64
    return q_pos[:, None] - (n * CMP_BLOCK + CMP_BLOCK - 1)[None, :]


def _layer(hp, hs_t, cache_l, pt_flat, win_l, pool_l, conv_l, wts, table, dims):
    (g_pre_mix, w_in, pe_k, w1_k, w2_k, pe_v, w1_v, w2_v, w_pool_grp, pool_scale, w_pool_proj, w_nsa_proj,
     w_out, g_post_mix, g_pre_ffn, w_up, conv_w, conv_b, w_down, g_post_ffn) = wts
    batch, seq, bs, ts, n_pages, page = dims
    d_model = hp.shape[1]
    pool_w = w_pool_grp.shape[0] * w_pool_grp.shape[1]
    q_w = N_HEADS * HEAD_DIM
    d_ff = w_down.shape[0]
    past_len = n_pages * page
    c_q, c_kv, c_ng, c_mg = pool_w, pool_w + q_w, pool_w + q_w + 6 * KVD, pool_w + q_w + 6 * KVD + 3 * N_HEADS
    assert pool_w == q_w == 4 * KVD, "column blocks below assume equal widths"

    row = lambda v: v.reshape(1, -1)
    w_a = w_in[:, :c_ng].astype(BF16)
    w_ng = jnp.pad(w_in[:, c_ng:c_mg], ((0, 0), (0, LANES - 3 * N_HEADS))).astype(BF16)
    w_mg = w_in[:, c_mg:].astype(BF16)
    w_grp = w_pool_grp.astype(BF16)
    w1bd = jnp.stack([_block_diag4(w1_k), _block_diag4(w1_v)]).astype(BF16)
    w2bd = jnp.stack([_block_diag4(w2_k), _block_diag4(w2_v)]).astype(BF16)
    chunk = 2 * page
    blk_chunk = chunk // CMP_BLOCK
    regroup = (jnp.arange(chunk) % blk_chunk) * CMP_BLOCK + jnp.arange(chunk) // blk_chunk
    perm = (jnp.arange(chunk)[None, :] == regroup[:, None]).astype(BF16)
    pe_n = jnp.stack([jnp.tile(pe, (blk_chunk, N_KV_HEADS)) for pe in (pe_k, pe_v)])
    pe_t = pe_n.transpose(0, 2, 1)
    wv, wg = w_up[:, :d_ff].astype(BF16), w_up[:, d_ff:].astype(BF16)
    wd = w_down.astype(BF16)
    wpp, wnp, wo = w_pool_proj.astype(BF16), w_nsa_proj.astype(BF16), w_out.astype(BF16)
    heads = table.reshape(N_BUCKETS, N_HEADS)

    z = rms_matmul(hp, row(g_pre_mix), w_a, tm=1024, tn=512, name="proj_main_prompt")
    gates = rms_matmul(hp, row(g_pre_mix), w_mg, tm=1024, tn=512, act="sigmoid", name="proj_gate_prompt")
    ng = rms_matmul(hp, row(g_pre_mix), w_ng, tm=1024, tn=LANES, act="sigmoid", name="proj_ng_prompt")
    a_p = pool_prompt(z, w_grp, row(pool_scale), batch=batch, seq=seq, tt=512)
    z3 = z.reshape(batch, seq, -1)
    kv_rows_p = z3[:, :, c_kv:c_kv + 4 * KVD].reshape(batch, seq, N_PAGED_SLOTS, N_KV_HEADS, HEAD_DIM)
    win_p = z3[:, seq - min(WINDOW, seq):, c_kv + 4 * KVD:c_ng].reshape(batch, -1, 2, N_KV_HEADS, HEAD_DIM)
    pool_new_p = z3[:, seq - POOL_HIST:, :pool_w]

    cmp_p = compress_prompt(z, c_kv // (2 * KVD), perm, pe_n, w1bd, w2bd, batch=batch, seq=seq)
    cmp_p = cmp_p.reshape(batch, 2, -1, N_KV_HEADS, HEAD_DIM).transpose(1, 0, 3, 2, 4)
    cmp_p = _cmp_lane_layout(cmp_p, 3).astype(BF16)
    q_h = z3[:, :, c_q:c_kv].reshape(batch, seq, N_HEADS, HEAD_DIM).transpose(0, 2, 1, 3).astype(BF16)
    kv_h = z3[:, :, c_kv:c_ng].reshape(batch, seq, 6, N_KV_HEADS, HEAD_DIM).transpose(2, 0, 3, 1, 4).astype(BF16)
    pos_p = jnp.arange(seq)
    bias_cmp = heads[_bucket(_cmp_dist(pos_p))].transpose(2, 0, 1)
    ocmp, sel = cmp_select_prompt(q_h, cmp_p[0], cmp_p[1], bias_cmp, seq=seq)
    tile_dist = TQ + jnp.arange(TQ)[:, None] - jnp.arange(2 * TQ)[None, :]
    bias_tile = heads[_bucket(tile_dist)].transpose(2, 0, 1)
    ng_h = ng[:, :3 * N_HEADS].reshape(batch, seq, 3, N_HEADS).transpose(0, 3, 1, 2)
    o_h = attn_prompt(heads[N_BUCKETS - 1], q_h, kv_h[2], kv_h[3], kv_h[4], kv_h[5], sel, ocmp, ng_h, bias_tile,
                      seq=seq)
    o_p = o_h.transpose(0, 2, 1, 3).reshape(batch * seq, q_w)
    hp = merge(a_p, o_p, gates, hp, wpp, wnp, wo, row(g_post_mix), tm=256)
    hp, tails = ffn(hp, None, row(g_pre_ffn), wv, wg, conv_w, row(conv_b), wd, row(g_post_ffn),
                    tm=512, tf=512, halo=16, shift=1, tiles_per_seq=seq // 512, tail=8)
    conv_new_p = tails.reshape(batch, seq // 512, 8, d_ff)[:, -1, 8 - (CONV_WIDTH - 1):]

    n_s = ts * bs
    zs = rms_matmul(hs_t, row(g_pre_mix), w_a, tm=n_s, tn=512, name="proj_main_sample")
    gates_s = rms_matmul(hs_t, row(g_pre_mix), w_mg, tm=n_s, tn=512, act="sigmoid", name="proj_gate_sample")
    ng_s = rms_matmul(hs_t, row(g_pre_mix), w_ng, tm=n_s, tn=LANES, act="sigmoid", name="proj_ng_sample")
    hist_t = pool_l.transpose(1, 0, 2)
    a_s = pool_sample(zs, hist_t, w_grp, row(pool_scale), ts=ts, bs=bs, past_len=past_len)
    zs3 = zs.reshape(ts, bs, -1)
    kv_rows_s = zs3[:, :, c_kv:c_kv + 4 * KVD].transpose(1, 0, 2).reshape(bs, ts, N_PAGED_SLOTS, N_KV_HEADS, HEAD_DIM)
    win_rows = zs3[:, :, c_kv + 4 * KVD:c_ng].transpose(1, 0, 2).reshape(bs, ts, 2, N_KV_HEADS, HEAD_DIM)
    win_new_s = jnp.concatenate([win_l, win_rows], axis=1)[:, -win_l.shape[1]:]
    pool_new_s = jnp.concatenate([pool_l, zs3[:, :, :pool_w].transpose(1, 0, 2)], axis=1)[:, -POOL_HIST:]

    cache_t = cache_l.transpose(0, 2, 3, 4, 1).reshape(cache_l.shape[0], N_PAGED_SLOTS * KVD, page)
    cmp_s = compress_sample(cache_t, pt_flat, perm, pe_t, w1bd, w2bd, bs=bs, n_pages=n_pages)
    cmp_s = _cmp_lane_layout(cmp_s, 2).astype(BF16)

    rows = GROUP * N_KV_HEADS * ts
    q_s = zs3[:, :, c_q:c_kv].reshape(ts, bs, N_KV_HEADS, GROUP, HEAD_DIM).transpose(1, 3, 2, 0, 4)
    q_bd = jnp.einsum("bgktd,kj->bgktjd", q_s, jnp.eye(N_KV_HEADS, dtype=F32)).reshape(bs, rows, KVD).astype(BF16)
    pad_rows = lambda x: jnp.pad(x.transpose(1, 0, 2), ((0, 0), (0, 16 - ts), (0, 0))).astype(BF16)
    knew = pad_rows(zs3[:, :, c_kv + 2 * KVD:c_kv + 4 * KVD])
    wnew = pad_rows(zs3[:, :, c_kv + 4 * KVD:c_ng])
    wb = win_l.shape[1]
    win_t = win_l.transpose(0, 2, 3, 4, 1).reshape(bs, 2 * KVD, wb)
    r_idx = jnp.arange(rows)
    r_head = ((r_idx % (N_KV_HEADS * ts)) // ts) * GROUP + r_idx // (N_KV_HEADS * ts)
    r_pos = past_len + r_idx % ts
    rbias = lambda dist: heads[_bucket(dist), r_head[:, None]]
    bcmp = rbias(_cmp_dist(r_pos))
    bsel = rbias(r_pos[:, None] - jnp.arange(past_len)[None, :])
    bnew = rbias((r_idx % ts)[:, None] - jnp.arange(16)[None, :])
    bwin = rbias(r_pos[:, None] - (past_len - wb + jnp.arange(wb))[None, :])
    expand = (jnp.arange(LANES)[:, None] == (jnp.arange(past_len) // SEL_BLOCK)[None, :]).astype(BF16)
    ng_rows = ng_s[:, :3 * N_HEADS].reshape(ts, bs, 3, N_KV_HEADS, GROUP).transpose(1, 4, 3, 0, 2).reshape(bs, rows, 3)
    o_s = attn_sample(cache_t, pt_flat, q_bd, knew, win_t, wnew, cmp_s[:, 0], cmp_s[:, 1], bcmp, bsel, bnew, bwin,
                      bnew, expand, ng_rows, bs=bs, ts=ts, n_pages=n_pages, past_len=past_len)
    o_s = o_s.reshape(bs, GROUP, N_KV_HEADS, ts, HEAD_DIM).transpose(3, 0, 2, 1, 4).reshape(n_s, q_w)
    hs_t = merge(a_s, o_s, gates_s, hs_t, wpp, wnp, wo, row(g_post_mix), tm=256)
    conv_hist = conv_l.transpose(1, 0, 2).reshape((CONV_WIDTH - 1) * bs, d_ff)
    hs_t, tails_s = ffn(hs_t, conv_hist, row(g_pre_ffn), wv, wg, conv_w, row(conv_b), wd, row(g_post_ffn),
                        tm=n_s, tf=512, halo=(CONV_WIDTH - 1) * bs, shift=bs, tiles_per_seq=1,
                        tail=(CONV_WIDTH - 1) * bs)
    conv_new_s = tails_s.reshape(CONV_WIDTH - 1, bs, d_ff).transpose(1, 0, 2)
    return hp, hs_t, (kv_rows_p, kv_rows_s, win_p, win_new_s, pool_new_p, pool_new_s, conv_new_p, conv_new_s)


def kernel(x_prompt, x_sample, cache_kv, page_table, state_kv_win, state_pool, state_conv, g_pre_mix, w_in, pe_cmp_k, w1_cmp_k, w2_cmp_k, pe_cmp_v, w1_cmp_v, w2_cmp_v, rel_bias, w_pool_grp, pool_scale, w_pool_proj, w_nsa_proj, w_out, g_post_mix, g_pre_ffn, w_up, conv_w, conv_b, w_down, g_post_ffn):
    batch, seq, d_model = x_prompt.shape
    bs, ts, _ = x_sample.shape
    depth = cache_kv.shape[0]
    n_pages, page = page_table.shape[1], cache_kv.shape[2]
    assert ts >= CONV_WIDTH - 1
    dims = (batch, seq, bs, ts, n_pages, page)
    pt_flat = page_table.reshape(-1).astype(jnp.int32)
    hp = x_prompt.reshape(batch * seq, d_model)
    hs_t = x_sample.transpose(1, 0, 2).reshape(ts * bs, d_model)
    outs = [[] for _ in range(8)]
    per_layer = (g_pre_mix, w_in, pe_cmp_k, w1_cmp_k, w2_cmp_k, pe_cmp_v, w1_cmp_v, w2_cmp_v, w_pool_grp, pool_scale,
                 w_pool_proj, w_nsa_proj, w_out, g_post_mix, g_pre_ffn, w_up, conv_w, conv_b, w_down, g_post_ffn)
    for l in range(depth):
        wts = tuple(w[l] for w in per_layer)
        hp, hs_t, states = _layer(hp, hs_t, cache_kv[l], pt_flat, state_kv_win[l], state_pool[l], state_conv[l],
                                  wts, rel_bias, dims)
        for acc, s in zip(outs, states):
            acc.append(s)
    y_prompt = hp.reshape(batch, seq, d_model)
    y_sample = hs_t.reshape(ts, bs, d_model).transpose(1, 0, 2)
    return (y_prompt, y_sample) + tuple(jnp.stack(o) for o in outs)
```

```python
import functools
import math

import jax
import jax.numpy as jnp
from jax import lax
from jax.experimental import pallas as pl
from jax.experimental.pallas import tpu as pltpu

F32 = jnp.float32
BF16 = jnp.bfloat16

EPS = 1e-6
POOL_WINDOWS = (2, 4, 8, 16)
POOL_HIST = max(POOL_WINDOWS) - 1
N_HEADS = 16
HEAD_DIM = 64
N_KV_HEADS = 4
GROUP = N_HEADS // N_KV_HEADS
KVD = N_KV_HEADS * HEAD_DIM
N_PAGED_SLOTS = 4
CMP_BLOCK = 32
SEL_BLOCK = 64
N_SELECT = 16
WINDOW = 512
FORCE_BONUS = 1.0e4
ATTN_SCALE = HEAD_DIM ** -0.5
N_BUCKETS = 32
MAX_DISTANCE = 128
CONV_WIDTH = 3
NEG = -1e30
LANES = 128
TQ = 256
SM_LANES = 128
V_ROWS = HEAD_DIM + 16
CMP_GROUP = 4

_NT = (((1,), (1,)), ((), ()))


def _cparams(sem, vmem_mb):
    return pltpu.CompilerParams(dimension_semantics=sem, vmem_limit_bytes=vmem_mb * 1024 * 1024)


def _rms(x, g):
    ms = jnp.mean(x * x, axis=-1, keepdims=True)
    return x * lax.rsqrt(ms + EPS) * g


def _dot(a, b):
    return jnp.dot(a, b, preferred_element_type=F32)


def _dot_nt(a, b):
    return lax.dot_general(a, b, _NT, preferred_element_type=F32)


def _bucket(dist):
    n = jnp.maximum(dist, 0)
    max_exact = N_BUCKETS // 2
    nf = jnp.maximum(n, 1).astype(F32)
    large = max_exact + (jnp.log(nf / max_exact) / math.log(MAX_DISTANCE / max_exact)
                         * (N_BUCKETS - max_exact)).astype(jnp.int32)
    large = jnp.minimum(large, N_BUCKETS - 1)
    return jnp.where(n < max_exact, n, large)


def _bias_lookup(tab, dist):
    bkt = _bucket(dist)[None]
    col = lambda k: tab[:, k].reshape((-1,) + (1,) * dist.ndim)
    out = jnp.broadcast_to(col(N_BUCKETS - 1), (tab.shape[0],) + dist.shape)
    for k in range(N_BUCKETS - 1):
        out = jnp.where(bkt == k, col(k), out)
    return out


def _bias_lookup_rows(tab, dist):
    bkt = _bucket(dist)
    out = jnp.broadcast_to(tab[:, N_BUCKETS - 1:], dist.shape)
    for k in range(N_BUCKETS - 1):
        out = jnp.where(bkt == k, tab[:, k:k + 1], out)
    return out


def _softmax_parts(s, mask):
    s = jnp.where(mask, s, NEG)
    m = jnp.max(s, axis=-1, keepdims=True)
    e = jnp.where(mask, jnp.exp(s - m), 0.0)
    d = jnp.sum(e, axis=-1, keepdims=True)
    return e, d


def _topk_mask(score, n_blocks, k):
    lane = lax.broadcasted_iota(jnp.int32, score.shape, 1)
    rank = jnp.zeros(score.shape, F32)
    for i in range(n_blocks):
        ci = score[:, i:i + 1]
        rank = rank + jnp.where(lane > i, jnp.where(ci >= score, 1.0, 0.0), jnp.where(ci > score, 1.0, 0.0))
    return jnp.where((rank < k) & (lane < n_blocks), 1.0, 0.0)


def _block_scores(imp, q_pos, n_sb):
    blk = lax.broadcasted_iota(jnp.int32, imp.shape, 1)
    cur = q_pos // SEL_BLOCK
    forced = (blk == 0) | (blk == cur) | (blk == cur - 1)
    valid = (blk * SEL_BLOCK <= q_pos) & (blk < n_sb)
    return jnp.where(valid, imp + FORCE_BONUS * forced.astype(F32), -jnp.inf)


def _rms_matmul_kernel(x_ref, g_ref, w_ref, o_ref, xn_ref, *, act):
    @pl.when(pl.program_id(1) == 0)
    def _():
        xn_ref[...] = _rms(x_ref[...], g_ref[...]).astype(BF16)

    acc = _dot(xn_ref[...], w_ref[...])
    if act == "sigmoid":
        acc = jax.nn.sigmoid(acc)
    o_ref[...] = acc.astype(o_ref.dtype)


def rms_matmul(x, g, w, *, tm, tn, act=None, name):
    n, k = x.shape
    m = w.shape[1]
    return pl.pallas_call(
        functools.partial(_rms_matmul_kernel, act=act),
        grid=(n // tm, m // tn),
        in_specs=[pl.BlockSpec((tm, k), lambda i, j: (i, 0)),
                  pl.BlockSpec((1, k), lambda i, j: (0, 0)),
                  pl.BlockSpec((k, tn), lambda i, j: (0, j))],
        out_specs=pl.BlockSpec((tm, tn), lambda i, j: (i, j)),
        out_shape=jax.ShapeDtypeStruct((n, m), F32),
        scratch_shapes=[pltpu.VMEM((tm, k), BF16)],
        compiler_params=_cparams(("arbitrary", "arbitrary"), 48),
        name=name,
    )(x, g, w)


def _pool_prompt_kernel(x_ref, h_ref, w_ref, sc_ref, o_ref, buf_ref, *, tt):
    i = pl.program_id(1)
    pg = w_ref.shape[1]
    buf_ref[0:16, :] = jnp.where(i == 0, 0.0, h_ref[...])
    buf_ref[16:, :] = x_ref[...]
    pos = i * tt + lax.broadcasted_iota(jnp.int32, (tt, 1), 0)
    for gi, w in enumerate(POOL_WINDOWS):
        c = slice(gi * pg, (gi + 1) * pg)
        x = buf_ref[16:, c]
        win = x
        for k in range(1, w):
            win = win + buf_ref[16 - k:16 - k + tt, c]
        cnt = jnp.minimum(w, pos + 1).astype(F32)
        pooled = win / cnt - x
        o_ref[:, c] = (_dot(pooled.astype(BF16), w_ref[gi]) * sc_ref[:, c]).astype(o_ref.dtype)


def pool_prompt(z, w_grp, scale, *, batch, seq, tt):
    pw = w_grp.shape[0] * w_grp.shape[1]
    nt = seq // tt
    return pl.pallas_call(
        functools.partial(_pool_prompt_kernel, tt=tt),
        grid=(batch, nt),
        in_specs=[pl.BlockSpec((tt, pw), lambda b, i: (b * nt + i, 0)),
                  pl.BlockSpec((16, pw), lambda b, i: (jnp.maximum((b * nt + i) * (tt // 16) - 1, 0), 0)),
                  pl.BlockSpec(w_grp.shape, lambda b, i: (0, 0, 0)),
                  pl.BlockSpec((1, pw), lambda b, i: (0, 0))],
        out_specs=pl.BlockSpec((tt, pw), lambda b, i: (b * nt + i, 0)),
        out_shape=jax.ShapeDtypeStruct((batch * seq, pw), BF16),
        scratch_shapes=[pltpu.VMEM((tt + 16, pw), F32)],
        compiler_params=_cparams(("arbitrary", "arbitrary"), 32),
        name="pool_prompt",
    )(z, z, w_grp, scale)


def _pool_sample_kernel(x_ref, h_ref, w_ref, sc_ref, o_ref, *, ts, bs, past_len):
    pg = w_ref.shape[1]

    def row(j, c):
        if j < POOL_HIST:
            return h_ref[j, :, c]
        return x_ref[(j - POOL_HIST) * bs:(j - POOL_HIST + 1) * bs, c]

    for t in range(ts):
        for gi, w in enumerate(POOL_WINDOWS):
            c = slice(gi * pg, (gi + 1) * pg)
            x = row(POOL_HIST + t, c)
            win = x
            for k in range(1, w):
                win = win + row(POOL_HIST + t - k, c)
            cnt = float(min(w, past_len + t + 1))
            pooled = win / cnt - x
            o_ref[t * bs:(t + 1) * bs, c] = (_dot(pooled.astype(BF16), w_ref[gi]) * sc_ref[:, c]).astype(o_ref.dtype)


def pool_sample(z, hist_t, w_grp, scale, *, ts, bs, past_len):
    pw = w_grp.shape[0] * w_grp.shape[1]
    return pl.pallas_call(
        functools.partial(_pool_sample_kernel, ts=ts, bs=bs, past_len=past_len),
        grid=(1,),
        in_specs=[pl.BlockSpec((ts * bs, pw), lambda i: (0, 0)),
                  pl.BlockSpec(hist_t.shape, lambda i: (0, 0, 0)),
                  pl.BlockSpec(w_grp.shape, lambda i: (0, 0, 0)),
                  pl.BlockSpec((1, pw), lambda i: (0, 0))],
        out_specs=pl.BlockSpec((ts * bs, pw), lambda i: (0, 0)),
        out_shape=jax.ShapeDtypeStruct((ts * bs, pw), BF16),
        compiler_params=_cparams(("arbitrary",), 48),
        name="pool_sample",
    )(z, hist_t, w_grp, scale)


def _compress_tail(hid, w2_ref):
    hid = jax.nn.gelu(hid, approximate=True)
    return _dot(hid.astype(BF16), w2_ref[...])


def _compress_prompt_kernel(x_ref, perm_ref, pe_ref, w1_ref, w2_ref, o_ref, xs_ref, *, n_blk, chunk):
    n_chunks = x_ref.shape[0] // chunk
    blk_chunk = chunk // CMP_BLOCK
    for s in range(2):
        c = slice(s * KVD, (s + 1) * KVD)
        for ch in range(n_chunks):
            x = (x_ref[ch * chunk:(ch + 1) * chunk, c] + pe_ref[s]).astype(BF16)
            xs_ref[ch] = _dot(perm_ref[...], x).reshape(CMP_BLOCK, blk_chunk, KVD)
        hid = jnp.zeros((n_blk, KVD), F32)
        for l in range(CMP_BLOCK):
            hid = hid + _dot(xs_ref[:, l].reshape(n_blk, KVD).astype(BF16), w1_ref[s, l])
        o_ref[0, s] = _compress_tail(hid, w2_ref.at[s])


def compress_prompt(z, col_block, perm, pe_n, w1bd, w2bd, *, batch, seq):
    n_blk = seq // CMP_BLOCK
    chunk = perm.shape[0]
    assert seq % chunk == 0
    return pl.pallas_call(
        functools.partial(_compress_prompt_kernel, n_blk=n_blk, chunk=chunk),
        grid=(batch,),
        in_specs=[pl.BlockSpec((seq, 2 * KVD), lambda b: (b, col_block)),
                  pl.BlockSpec(perm.shape, lambda b: (0, 0)),
                  pl.BlockSpec(pe_n.shape, lambda b: (0, 0, 0)),
                  pl.BlockSpec(w1bd.shape, lambda b: (0, 0, 0, 0), pipeline_mode=pl.Buffered(1)),
                  pl.BlockSpec(w2bd.shape, lambda b: (0, 0, 0))],
        out_specs=pl.BlockSpec((1, 2, n_blk, KVD), lambda b: (b, 0, 0, 0)),
        out_shape=jax.ShapeDtypeStruct((batch, 2, n_blk, KVD), F32),
        scratch_shapes=[pltpu.VMEM((seq // chunk, CMP_BLOCK, chunk // CMP_BLOCK, KVD), F32)],
        compiler_params=_cparams(("arbitrary",), 48),
        name="compress_prompt",
    )(z, perm, pe_n, w1bd, w2bd)


def _compress_sample_kernel(pt_ref, *refs, n_pages, page):
    del pt_ref
    page_refs = refs[:n_pages]
    perm_ref, pe_ref, w1_ref, w2_ref, o_ref, x_ref = refs[n_pages:]
    b = pl.program_id(0)
    bl = b % CMP_GROUP
    n_pairs = n_pages // 2
    blk_pair = 2 * page // CMP_BLOCK
    for pp in range(n_pairs):
        for s in range(2):
            r = slice(s * KVD, (s + 1) * KVD)
            xt = jnp.concatenate([page_refs[2 * pp][0, r, :], page_refs[2 * pp + 1][0, r, :]], axis=1)
            xt = (xt + pe_ref[s]).astype(BF16)
            xp = _dot_nt(perm_ref[...], xt)
            x_ref[s, bl * n_pairs + pp] = xp.reshape(CMP_BLOCK, blk_pair, KVD)

    @pl.when(bl == CMP_GROUP - 1)
    def _():
        rows = CMP_GROUP * n_pairs * blk_pair
        for s in range(2):
            hid = jnp.zeros((rows, KVD), F32)
            for l in range(CMP_BLOCK):
                xl = x_ref[s, :, l].reshape(rows, KVD)
                hid = hid + _dot(xl.astype(BF16), w1_ref[s, l])
            o_ref[:, s] = _compress_tail(hid, w2_ref.at[s]).reshape(CMP_GROUP, n_pairs * blk_pair, KVD)


def compress_sample(cache_t, pt_flat, perm, pe_t, w1bd, w2bd, *, bs, n_pages):
    page = cache_t.shape[2]
    assert page == LANES and n_pages % 2 == 0 and bs % CMP_GROUP == 0
    n_pairs = n_pages // 2
    blk_pair = 2 * page // CMP_BLOCK
    n_blk = n_pairs * blk_pair

    def page_map(b, pt, *, p):
        return (pt[b * n_pages + p], 0, 0)

    const = lambda nd: (lambda b, pt: (0,) * nd)
    grid_spec = pltpu.PrefetchScalarGridSpec(
        num_scalar_prefetch=1,
        grid=(bs,),
        in_specs=[pl.BlockSpec((1, 2 * KVD, page), functools.partial(page_map, p=p)) for p in range(n_pages)]
        + [pl.BlockSpec(perm.shape, const(2)), pl.BlockSpec(pe_t.shape, const(3)),
           pl.BlockSpec(w1bd.shape, const(4), pipeline_mode=pl.Buffered(1)), pl.BlockSpec(w2bd.shape, const(3))],
        out_specs=pl.BlockSpec((CMP_GROUP, 2, n_blk, KVD), lambda b, pt: (b // CMP_GROUP, 0, 0, 0)),
        scratch_shapes=[pltpu.VMEM((2, CMP_GROUP * n_pairs, CMP_BLOCK, blk_pair, KVD), F32)],
    )
    return pl.pallas_call(
        functools.partial(_compress_sample_kernel, n_pages=n_pages, page=page),
        grid_spec=grid_spec,
        out_shape=jax.ShapeDtypeStruct((bs, 2, n_blk, KVD), F32),
        compiler_params=_cparams(("arbitrary",), 56),
        name="compress_sample",
    )(pt_flat, *([cache_t] * n_pages), perm, pe_t, w1bd, w2bd)


def _cmp_valid(q_pos, shape, n_cmp):
    lane = lax.broadcasted_iota(jnp.int32, shape, 1)
    j = lane % 64
    n = 2 * j + lane // 64
    return (n < n_cmp) & (n * CMP_BLOCK + CMP_BLOCK - 1 <= q_pos)


def _cmp_select_prompt_kernel(q_ref, kc_ref, vc_ref, bias_ref, ocmp_ref, sel_ref, *, n_cmp, n_sb):
    qb = pl.program_id(2)
    q_pos = qb * TQ + lax.broadcasted_iota(jnp.int32, (TQ, 1), 0)
    valid = _cmp_valid(q_pos, (TQ, LANES), n_cmp)
    kc = kc_ref[0, 0]
    vc = vc_ref[0, 0]
    imp = jnp.zeros((TQ, LANES), F32)
    for g in range(GROUP):
        s = _dot_nt(q_ref[0, g], kc) + bias_ref[g]
        e, d = _softmax_parts(s, valid)
        p = e / jnp.where(d > 0, d, 1.0)
        ocmp_ref[0, g] = _dot(p.astype(BF16), vc)
        imp = imp + p
    imp = imp + pltpu.roll(imp, 64, axis=1)
    score = _block_scores(imp, q_pos, n_sb)
    sel_ref[0, 0] = _topk_mask(score, n_sb, min(N_SELECT, n_sb))


def cmp_select_prompt(q_h, kc_p, vc_p, bias_cmp, *, seq):
    batch = q_h.shape[0]
    n_cmp = seq // CMP_BLOCK
    n_sb = -(-seq // SEL_BLOCK)
    return pl.pallas_call(
        functools.partial(_cmp_select_prompt_kernel, n_cmp=n_cmp, n_sb=n_sb),
        grid=(batch, N_KV_HEADS, seq // TQ),
        in_specs=[pl.BlockSpec((1, GROUP, TQ, HEAD_DIM), lambda b, kv, i: (b, kv, i, 0)),
                  pl.BlockSpec((1, 1, LANES, HEAD_DIM), lambda b, kv, i: (b, kv, 0, 0)),
                  pl.BlockSpec((1, 1, LANES, HEAD_DIM), lambda b, kv, i: (b, kv, 0, 0)),
                  pl.BlockSpec((GROUP, TQ, LANES), lambda b, kv, i: (kv, i, 0))],
        out_specs=[pl.BlockSpec((1, GROUP, TQ, HEAD_DIM), lambda b, kv, i: (b, kv, i, 0)),
                   pl.BlockSpec((1, 1, TQ, LANES), lambda b, kv, i: (b, kv, i, 0))],
        out_shape=[jax.ShapeDtypeStruct((batch, N_HEADS, seq, HEAD_DIM), F32),
                   jax.ShapeDtypeStruct((batch, N_KV_HEADS, seq, LANES), F32)],
        compiler_params=_cparams(("arbitrary",) * 3, 32),
        name="cmp_select_prompt",
    )(q_h, kc_p, vc_p, bias_cmp)


def _attn_prompt_kernel(far_ref, qt_ref, ks_ref, vst_ref, kw_ref, vwt_ref, sel_ref, ocmp_ref, ng_ref, bt_ref, o_ref,
                        s_ref, p_ref, term_ref, m_ref, a_ref, acc_ref, osel_ref):
    kv = pl.program_id(1)
    qb = pl.program_id(2)
    qt = qt_ref[0, 0, 0]
    selm = sel_ref[0, 0].astype(BF16)
    tok_minus_key = lax.broadcasted_iota(jnp.int32, (TQ, TQ), 1) - lax.broadcasted_iota(jnp.int32, (TQ, TQ), 0)
    blk_of_key = lax.broadcasted_iota(jnp.int32, (TQ, LANES), 0) // SEL_BLOCK
    blk_lane = lax.broadcasted_iota(jnp.int32, (TQ, LANES), 1)
    near0 = jnp.maximum(qb - 1, 0)

    def chunk(j, k_ref, vt_ref, term, near):
        start = pl.multiple_of(j * TQ, TQ)
        s_ref[...] = _dot(k_ref[0, 0, pl.ds(start, TQ), :], qt)
        term_ref[...] = term
        for g in range(GROUP):
            far = far_ref[kv * GROUP + g]
            for half in range(TQ // SM_LANES):
                c = slice(g * TQ + half * SM_LANES, g * TQ + (half + 1) * SM_LANES)
                t = slice(half * SM_LANES, (half + 1) * SM_LANES)
                u = s_ref[:, c] + term_ref[:, t]
                if near:
                    u = u + bt_ref[g, j - (qb - 1), :, t]
                m_old = m_ref[:, c]
                mx = jnp.max(u, axis=0, keepdims=True)
                m_new = jnp.maximum(m_old, mx if near else mx + far)
                a_ref[:, c] = jnp.exp(m_old - m_new)
                m_ref[:, c] = m_new
                p_ref[:, c] = jnp.exp(u - (m_new if near else m_new - far)).astype(BF16)
        acc_ref[...] = a_ref[...] * acc_ref[...] + _dot(vt_ref[0, 0, j], p_ref[...])

    def sel_term(j):
        expand = (blk_lane == blk_of_key + j * (TQ // SEL_BLOCK)).astype(BF16)
        return (_dot_nt(expand, selm) - 1.0) * (-NEG)

    def causal_term(j):
        return jnp.where(tok_minus_key + (qb - j) * TQ >= 0, 0.0, NEG)

    def run(k_ref, vt_ref, selected):
        m_ref[...] = jnp.full(m_ref.shape, NEG, F32)
        acc_ref[...] = jnp.zeros(acc_ref.shape, F32)
        if selected:
            def far_body(j, c):
                chunk(j, k_ref, vt_ref, sel_term(j), False)
                return c
            lax.fori_loop(0, near0, far_body, 0)
        else:
            @pl.when(qb >= WINDOW // TQ)
            def _():
                chunk(qb - WINDOW // TQ, k_ref, vt_ref, jnp.where(tok_minus_key <= 0, 0.0, NEG), False)

        def near_body(j, c):
            term = causal_term(j) + sel_term(j) if selected else causal_term(j)
            chunk(j, k_ref, vt_ref, term, True)
            return c
        lax.fori_loop(near0, qb + 1, near_body, 0)
        l = acc_ref[HEAD_DIM:HEAD_DIM + 1, :]
        return acc_ref[0:HEAD_DIM, :] / jnp.where(l > 0, l, 1.0)

    osel_ref[...] = run(ks_ref, vst_ref, True)
    o_win = run(kw_ref, vwt_ref, False)
    ng = ng_ref[0, 0, 0]
    o = ng[0:1, :] * ocmp_ref[0, 0, 0] + ng[1:2, :] * osel_ref[...] + ng[2:3, :] * o_win
    o_ref[0, 0, 0] = o.astype(o_ref.dtype)


def attn_prompt(far, q_t, ks, vs_t, kw, vw_t, sel, ocmp_t, ng_t, bias_tile, *, seq):
    batch = q_t.shape[0]
    assert WINDOW == 2 * TQ and seq % TQ == 0 and TQ % SM_LANES == 0
    cols = GROUP * TQ
    nq = seq // TQ
    k_spec = pl.BlockSpec((1, 1, seq, HEAD_DIM), lambda b, kv, i: (b, kv, 0, 0))
    vt_spec = pl.BlockSpec((1, 1, nq, V_ROWS, TQ), lambda b, kv, i: (b, kv, 0, 0, 0))
    tile_spec = lambda r: pl.BlockSpec((1, 1, 1, r, cols), lambda b, kv, i: (b, kv, i, 0, 0))
    return pl.pallas_call(
        _attn_prompt_kernel,
        grid=(batch, N_KV_HEADS, nq),
        in_specs=[pl.BlockSpec(memory_space=pltpu.SMEM),
                  tile_spec(HEAD_DIM), k_spec, vt_spec, k_spec, vt_spec,
                  pl.BlockSpec((1, 1, TQ, LANES), lambda b, kv, i: (b, kv, i, 0)),
                  tile_spec(HEAD_DIM), tile_spec(8),
                  pl.BlockSpec((GROUP, 2, TQ, TQ), lambda b, kv, i: (kv, 0, 0, 0))],
        out_specs=tile_spec(HEAD_DIM),
        out_shape=jax.ShapeDtypeStruct((batch, N_KV_HEADS, nq, HEAD_DIM, cols), BF16),
        scratch_shapes=[pltpu.VMEM((TQ, cols), F32), pltpu.VMEM((TQ, cols), BF16), pltpu.VMEM((TQ, TQ), F32),
                        pltpu.VMEM((1, cols), F32), pltpu.VMEM((1, cols), F32),
                        pltpu.VMEM((V_ROWS, cols), F32), pltpu.VMEM((HEAD_DIM, cols), F32)],
        compiler_params=_cparams(("arbitrary",) * 3, 40),
        name="attn_prompt",
    )(far, q_t, ks, vs_t, kw, vw_t, sel, ocmp_t, ng_t, bias_tile)


def _diag_heads(x, ts):
    rows = x.shape[0]
    kv_of_row = (lax.broadcasted_iota(jnp.int32, (rows, HEAD_DIM), 0) % (N_KV_HEADS * ts)) // ts
    out = jnp.zeros((rows, HEAD_DIM), F32)
    for kv in range(N_KV_HEADS):
        out = out + jnp.where(kv_of_row == kv, x[:, kv * HEAD_DIM:(kv + 1) * HEAD_DIM], 0.0)
    return out


def _attn_sample_kernel(pt_ref, *refs, n_pages, page, ts, past_len, wb):
    del pt_ref
    page_refs = refs[:n_pages]
    (q_ref, knew_ref, win_ref, wnew_ref, kc_ref, vc_ref, bcmp_ref, bsel_ref, bseln_ref, bwin_ref, bwinn_ref,
     expand_ref, ng_ref, o_ref, kt_ref, vt_ref) = refs[n_pages:]
    rows = GROUP * N_KV_HEADS * ts
    kvt = N_KV_HEADS * ts
    n_cmp = past_len // CMP_BLOCK
    n_sb = -(-(past_len + ts) // SEL_BLOCK)
    q = q_ref[0]
    t_of_row = lax.broadcasted_iota(jnp.int32, (rows, 1), 0) % ts
    q_pos = past_len + t_of_row

    s = _dot_nt(q, kc_ref[0]) + bcmp_ref[...]
    e, d = _softmax_parts(s, _cmp_valid(q_pos, (rows, LANES), n_cmp))
    p = e / jnp.where(d > 0, d, 1.0)
    o_cmp = _diag_heads(_dot(p.astype(BF16), vc_ref[0]), ts)
    imp = p[0:kvt]
    for g in range(1, GROUP):
        imp = imp + p[g * kvt:(g + 1) * kvt]
    imp = imp + pltpu.roll(imp, 64, axis=1)
    sel = _topk_mask(_block_scores(imp, q_pos[0:kvt], n_sb), n_sb, min(N_SELECT, n_sb))
    sel = jnp.concatenate([sel] * GROUP, axis=0)

    for pg in range(n_pages):
        kt_ref[:, pg * page:(pg + 1) * page] = page_refs[pg][0, 0:KVD, :].astype(BF16)
        vt_ref[:, pg * page:(pg + 1) * page] = page_refs[pg][0, KVD:2 * KVD, :].astype(BF16)
    new_col = lax.broadcasted_iota(jnp.int32, (rows, 16), 1)
    new_ok = (new_col <= t_of_row) & (new_col < ts)
    s = _dot(q, kt_ref[...]) + bsel_ref[...]
    mask = _dot(sel.astype(BF16), expand_ref[...]) > 0.5
    sn = _dot_nt(q, knew_ref[0, :, 0:KVD]) + bseln_ref[...]
    mask_n = new_ok & (sel[:, past_len // SEL_BLOCK:past_len // SEL_BLOCK + 1] > 0.5)
    o_sel = _two_part_attention(s, mask, vt_ref[...], sn, mask_n, knew_ref[0, :, KVD:2 * KVD], ts)

    s = _dot(q, win_ref[0, 0:KVD, :].astype(BF16)) + bwin_ref[...]
    dist = wb + t_of_row - lax.broadcasted_iota(jnp.int32, (rows, wb), 1)
    mask = (dist >= 0) & (dist <= WINDOW)
    sn = _dot_nt(q, wnew_ref[0, :, 0:KVD]) + bwinn_ref[...]
    o_win = _two_part_attention(s, mask, win_ref[0, KVD:2 * KVD, :].astype(BF16), sn, new_ok,
                                wnew_ref[0, :, KVD:2 * KVD], ts)

    ng = ng_ref[0]
    o_ref[0] = (ng[:, 0:1] * o_cmp + ng[:, 1:2] * o_sel + ng[:, 2:3] * o_win).astype(o_ref.dtype)


def _two_part_attention(s, mask, vt, sn, mask_n, vn, ts):
    s = jnp.where(mask, s, NEG)
    sn = jnp.where(mask_n, sn, NEG)
    m = jnp.maximum(jnp.max(s, axis=-1, keepdims=True), jnp.max(sn, axis=-1, keepdims=True))
    e = jnp.where(mask, jnp.exp(s - m), 0.0)
    en = jnp.where(mask_n, jnp.exp(sn - m), 0.0)
    d = jnp.sum(e, axis=-1, keepdims=True) + jnp.sum(en, axis=-1, keepdims=True)
    inv = jnp.where(d > 0, d, 1.0)
    o = _dot_nt((e / inv).astype(BF16), vt) + _dot((en / inv).astype(BF16), vn)
    return _diag_heads(o, ts)


def attn_sample(cache_t, pt_flat, q_bd, knew, win_t, wnew, kc_s, vc_s, bcmp, bsel, bseln, bwin, bwinn, expand, ng_s,
                *, bs, ts, n_pages, past_len):
    page = cache_t.shape[2]
    wb = win_t.shape[2]
    rows = GROUP * N_KV_HEADS * ts
    assert past_len == n_pages * page and past_len % SEL_BLOCK == 0 and ts <= 16
    assert (past_len + ts) // CMP_BLOCK == past_len // CMP_BLOCK

    def page_map(b, pt, *, p):
        return (pt[b * n_pages + p], 1, 0)

    per_b = lambda shape: pl.BlockSpec((1,) + shape, lambda b, pt: (b,) + (0,) * len(shape))
    const = lambda a: pl.BlockSpec(a.shape, lambda b, pt: (0,) * a.ndim)
    grid_spec = pltpu.PrefetchScalarGridSpec(
        num_scalar_prefetch=1,
        grid=(bs,),
        in_specs=[pl.BlockSpec((1, 2 * KVD, page), functools.partial(page_map, p=p)) for p in range(n_pages)]
        + [per_b((rows, KVD)), per_b((16, 2 * KVD)), per_b((2 * KVD, wb)), per_b((16, 2 * KVD)),
           per_b((LANES, KVD)), per_b((LANES, KVD)),
           const(bcmp), const(bsel), const(bseln), const(bwin), const(bwinn), const(expand),
           per_b((rows, 3))],
        out_specs=per_b((rows, HEAD_DIM)),
        scratch_shapes=[pltpu.VMEM((KVD, past_len), BF16), pltpu.VMEM((KVD, past_len), BF16)],
    )
    return pl.pallas_call(
        functools.partial(_attn_sample_kernel, n_pages=n_pages, page=page, ts=ts, past_len=past_len, wb=wb),
        grid_spec=grid_spec,
        out_shape=jax.ShapeDtypeStruct((bs, rows, HEAD_DIM), BF16),
        compiler_params=_cparams(("arbitrary",), 48),
        name="attn_sample",
    )(pt_flat, *([cache_t] * n_pages), q_bd, knew, win_t, wnew, kc_s, vc_s, bcmp, bsel, bseln, bwin, bwinn, expand,
      ng_s)


def _merge_kernel(a_ref, o_ref, g0_ref, g1_ref, h_ref, wp_ref, wn_ref, wo_ref, gp_ref, out_ref):
    merged = g0_ref[...] * _dot(a_ref[...], wp_ref[...]) + g1_ref[...] * _dot(o_ref[...], wn_ref[...])
    r = _dot(merged.astype(BF16), wo_ref[...])
    out_ref[...] = h_ref[...] + _rms(r, gp_ref[...])


def merge(a, o, gates, h, wp, wn, wo, gp):
    n, d = h.shape
    tm = min(256, n)
    pw = a.shape[1]
    qw = o.shape[1]
    once = pl.Buffered(1)
    return pl.pallas_call(
        _merge_kernel,
        grid=(n // tm,),
        in_specs=[pl.BlockSpec((tm, pw), lambda i: (i, 0)),
                  pl.BlockSpec((tm, qw), lambda i: (i, 0)),
                  pl.BlockSpec((tm, d), lambda i: (i, 0)),
                  pl.BlockSpec((tm, d), lambda i: (i, 1)),
                  pl.BlockSpec((tm, d), lambda i: (i, 0)),
                  pl.BlockSpec(wp.shape, lambda i: (0, 0), pipeline_mode=once),
                  pl.BlockSpec(wn.shape, lambda i: (0, 0), pipeline_mode=once),
                  pl.BlockSpec(wo.shape, lambda i: (0, 0), pipeline_mode=once),
                  pl.BlockSpec((1, d), lambda i: (0, 0))],
        out_specs=pl.BlockSpec((tm, d), lambda i: (i, 0)),
        out_shape=jax.ShapeDtypeStruct((n, d), F32),
        compiler_params=_cparams(("arbitrary",), 56),
        name="merge",
    )(a, o, gates, gates, h, wp, wn, wo, gp)


def _ffn_kernel(h_ref, halo_ref, gpre_ref, wv_ref, wg_ref, cw_ref, cb_ref, wd_ref, gpost_ref,
                out_ref, tail_ref, xn_ref, acc_ref, gbuf_ref, *, tm, halo, shift, tiles_per_seq, tail, from_rows):
    i = pl.program_id(0)
    f = pl.program_id(1)

    @pl.when(f == 0)
    def _():
        xn_ref[halo:, :] = _rms(h_ref[...], gpre_ref[...]).astype(BF16)
        if from_rows:
            xn_ref[0:halo, :] = _rms(halo_ref[...], gpre_ref[...]).astype(BF16)
        acc_ref[...] = jnp.zeros_like(acc_ref)

    val = _dot(xn_ref[halo:, :], wv_ref[...])
    if from_rows:
        gate_all = _dot(xn_ref[...], wg_ref[...])
        gbuf_ref[0:halo, :] = jnp.where(i % tiles_per_seq == 0, 0.0, gate_all[0:halo])
        gate = gate_all[halo:]
    else:
        gbuf_ref[0:halo, :] = halo_ref[...]
        gate = _dot(xn_ref[halo:, :], wg_ref[...])
    gbuf_ref[halo:, :] = gate
    conv = cb_ref[...]
    for j in range(CONV_WIDTH):
        back = (CONV_WIDTH - 1 - j) * shift
        conv = conv + cw_ref[j:j + 1, :] * gbuf_ref[halo - back:halo - back + tm, :]
    act = jax.nn.gelu(conv, approximate=True) * val
    acc_ref[...] += _dot(act.astype(BF16), wd_ref[...])
    tail_ref[...] = gate[tm - tail:, :]

    @pl.when(f == pl.num_programs(1) - 1)
    def _():
        out_ref[...] = h_ref[...] + _rms(acc_ref[...], gpost_ref[...])


def ffn(h, halo_src, gpre, wv, wg, cw, cb, wd, gpost, *, tm, tf, halo, shift, tiles_per_seq, tail):
    n, d = h.shape
    dff = wv.shape[1]
    nt = n // tm
    from_rows = halo_src is None
    if from_rows:
        halo_src = h
        halo_spec = pl.BlockSpec((halo, d), lambda i, f: (jnp.maximum(i * (tm // halo) - 1, 0), 0))
    else:
        assert nt == 1
        halo_spec = pl.BlockSpec((halo, tf), lambda i, f: (0, f))
    return pl.pallas_call(
        functools.partial(_ffn_kernel, tm=tm, halo=halo, shift=shift, tiles_per_seq=tiles_per_seq, tail=tail,
                          from_rows=from_rows),
        grid=(nt, dff // tf),
        in_specs=[pl.BlockSpec((tm, d), lambda i, f: (i, 0)),
                  halo_spec,
                  pl.BlockSpec((1, d), lambda i, f: (0, 0)),
                  pl.BlockSpec((d, tf), lambda i, f: (0, f)),
                  pl.BlockSpec((d, tf), lambda i, f: (0, f)),
                  pl.BlockSpec((CONV_WIDTH, tf), lambda i, f: (0, f)),
                  pl.BlockSpec((1, tf), lambda i, f: (0, f)),
                  pl.BlockSpec((tf, d), lambda i, f: (f, 0)),
                  pl.BlockSpec((1, d), lambda i, f: (0, 0))],
        out_specs=[pl.BlockSpec((tm, d), lambda i, f: (i, 0)),
                   pl.BlockSpec((tail, tf), lambda i, f: (i, f))],
        out_shape=[jax.ShapeDtypeStruct((n, d), F32),
                   jax.ShapeDtypeStruct((nt * tail, dff), F32)],
        scratch_shapes=[pltpu.VMEM((tm + halo, d), BF16), pltpu.VMEM((tm, d), F32),
                        pltpu.VMEM((tm + halo, tf), F32)],
        compiler_params=_cparams(("arbitrary", "arbitrary"), 56),
        name="ffn",
    )(h, halo_src, gpre, wv, wg, cw, cb, wd, gpost)


def _block_diag4(w):
    eye = jnp.eye(N_KV_HEADS, dtype=w.dtype)
    out = jnp.einsum("gh,...ab->...gahb", eye, w)
    return out.reshape(w.shape[:-2] + (N_KV_HEADS * w.shape[-2], N_KV_HEADS * w.shape[-1]))


def _cmp_lane_layout(x, axis):
    n = x.shape[axis]
    x = jnp.moveaxis(x, axis, 0)
    ev, od = x[0::2], x[1::2]
    pad = lambda y: jnp.pad(y, ((0, 64 - y.shape[0]),) + ((0, 0),) * (y.ndim - 1))
    return jnp.moveaxis(jnp.concatenate([pad(ev), pad(od)], axis=0), 0, axis)


def _cmp_dist(q_pos):
    lane = jnp.arange(LANES)
    n = 2 * (lane % 64) + lane // 64
    return q_pos[:, None] - (n * CMP_BLOCK + CMP_BLOCK - 1)[None, :]


def _layer(hp, hs_t, cache_l, pt_flat, win_l, pool_l, conv_l, wts, table, dims):
    (g_pre_mix, w_in, pe_k, w1_k, w2_k, pe_v, w1_v, w2_v, w_pool_grp, pool_scale, w_pool_proj, w_nsa_proj,
     w_out, g_post_mix, g_pre_ffn, w_up, conv_w, conv_b, w_down, g_post_ffn) = wts
    batch, seq, bs, ts, n_pages, page = dims
    d_model = hp.shape[1]
    pool_w = w_pool_grp.shape[0] * w_pool_grp.shape[1]
    q_w = N_HEADS * HEAD_DIM
    d_ff = w_down.shape[0]
    past_len = n_pages * page
    c_q, c_kv, c_ng, c_mg = pool_w, pool_w + q_w, pool_w + q_w + 6 * KVD, pool_w + q_w + 6 * KVD + 3 * N_HEADS
    assert pool_w == q_w == 4 * KVD, "column blocks below assume equal widths"

    row = lambda v: v.reshape(1, -1)
    w_a = w_in[:, :c_ng].astype(BF16)
    w_ng = jnp.pad(w_in[:, c_ng:c_mg], ((0, 0), (0, LANES - 3 * N_HEADS))).astype(BF16)
    w_mg = w_in[:, c_mg:].astype(BF16)
    w_grp = w_pool_grp.astype(BF16)
    w1bd = jnp.stack([_block_diag4(w1_k), _block_diag4(w1_v)]).astype(BF16)
    w2bd = jnp.stack([_block_diag4(w2_k), _block_diag4(w2_v)]).astype(BF16)
    chunk = 2 * page
    blk_chunk = chunk // CMP_BLOCK
    regroup = (jnp.arange(chunk) % blk_chunk) * CMP_BLOCK + jnp.arange(chunk) // blk_chunk
    perm = (jnp.arange(chunk)[None, :] == regroup[:, None]).astype(BF16)
    pe_n = jnp.stack([jnp.tile(pe, (blk_chunk, N_KV_HEADS)) for pe in (pe_k, pe_v)])
    pe_t = pe_n.transpose(0, 2, 1)
    wv, wg = w_up[:, :d_ff].astype(BF16), w_up[:, d_ff:].astype(BF16)
    wd = w_down.astype(BF16)
    wpp, wnp, wo = w_pool_proj.astype(BF16), w_nsa_proj.astype(BF16), w_out.astype(BF16)
    heads = table.reshape(N_BUCKETS, N_HEADS)

    z = rms_matmul(hp, row(g_pre_mix), w_a, tm=1024, tn=512, name="proj_main_prompt")
    gates = rms_matmul(hp, row(g_pre_mix), w_mg, tm=1024, tn=512, act="sigmoid", name="proj_gate_prompt")
    ng = rms_matmul(hp, row(g_pre_mix), w_ng, tm=1024, tn=LANES, act="sigmoid", name="proj_ng_prompt")
    a_p = pool_prompt(z, w_grp, row(pool_scale), batch=batch, seq=seq, tt=512)
    z3 = z.reshape(batch, seq, -1)
    kv_rows_p = z3[:, :, c_kv:c_kv + 4 * KVD].reshape(batch, seq, N_PAGED_SLOTS, N_KV_HEADS, HEAD_DIM)
    win_p = z3[:, seq - min(WINDOW, seq):, c_kv + 4 * KVD:c_ng].reshape(batch, -1, 2, N_KV_HEADS, HEAD_DIM)
    pool_new_p = z3[:, seq - POOL_HIST:, :pool_w]

    cmp_p = compress_prompt(z, c_kv // (2 * KVD), perm, pe_n, w1bd, w2bd, batch=batch, seq=seq)
    cmp_p = cmp_p.reshape(batch, 2, -1, N_KV_HEADS, HEAD_DIM).transpose(1, 0, 3, 2, 4)
    cmp_p = _cmp_lane_layout(cmp_p, 3).astype(BF16)
    q_h = (z3[:, :, c_q:c_kv] * ATTN_SCALE).reshape(batch, seq, N_HEADS, HEAD_DIM).transpose(0, 2, 1, 3).astype(BF16)
    kv_h = z3[:, :, c_kv:c_ng].reshape(batch, seq, 6, N_KV_HEADS, HEAD_DIM).transpose(2, 0, 3, 1, 4).astype(BF16)
    pos_p = jnp.arange(seq)
    bias_cmp = _bias_lookup(heads.T, _cmp_dist(pos_p))
    ocmp, sel = cmp_select_prompt(q_h, cmp_p[0], cmp_p[1], bias_cmp, seq=seq)
    tile_dist = (jnp.array([TQ, 0])[:, None, None] + jnp.arange(TQ)[None, None, :] - jnp.arange(TQ)[None, :, None])
    bias_tile = _bias_lookup(heads.T, tile_dist)
    nq = seq // TQ
    to_tiles = lambda x: (x.reshape(batch, N_KV_HEADS, GROUP, nq, TQ, x.shape[-1]).transpose(0, 1, 3, 5, 2, 4)
                          .reshape(batch, N_KV_HEADS, nq, x.shape[-1], GROUP * TQ))
    q_t = to_tiles(q_h)
    ocmp_t = to_tiles(ocmp)
    ng_h = ng[:, :3 * N_HEADS].reshape(batch, seq, 3, N_HEADS).transpose(0, 3, 1, 2)
    ng_t = jnp.pad(to_tiles(ng_h), ((0, 0), (0, 0), (0, 0), (0, 8 - 3), (0, 0)))
    ones_pad = jnp.zeros((V_ROWS - HEAD_DIM, TQ), BF16).at[0].set(1.0)
    to_vt = lambda v: jnp.concatenate(
        [v.reshape(batch, N_KV_HEADS, nq, TQ, HEAD_DIM).transpose(0, 1, 2, 4, 3),
         jnp.broadcast_to(ones_pad, (batch, N_KV_HEADS, nq) + ones_pad.shape)], axis=3)
    o_t = attn_prompt(heads[N_BUCKETS - 1], q_t, kv_h[2], to_vt(kv_h[3]), kv_h[4], to_vt(kv_h[5]), sel, ocmp_t, ng_t,
                      bias_tile, seq=seq)
    o_p = (o_t.reshape(batch, N_KV_HEADS, nq, HEAD_DIM, GROUP, TQ).transpose(0, 2, 5, 1, 4, 3)
           .reshape(batch * seq, q_w))
    hp = merge(a_p, o_p, gates, hp, wpp, wnp, wo, row(g_post_mix))
    hp, tails = ffn(hp, None, row(g_pre_ffn), wv, wg, conv_w, row(conv_b), wd, row(g_post_ffn),
                    tm=512, tf=512, halo=16, shift=1, tiles_per_seq=seq // 512, tail=8)
    conv_new_p = tails.reshape(batch, seq // 512, 8, d_ff)[:, -1, 8 - (CONV_WIDTH - 1):]

    n_s = ts * bs
    zs = rms_matmul(hs_t, row(g_pre_mix), w_a, tm=n_s, tn=512, name="proj_main_sample")
    gates_s = rms_matmul(hs_t, row(g_pre_mix), w_mg, tm=n_s, tn=512, act="sigmoid", name="proj_gate_sample")
    ng_s = rms_matmul(hs_t, row(g_pre_mix), w_ng, tm=n_s, tn=LANES, act="sigmoid", name="proj_ng_sample")
    hist_t = pool_l.transpose(1, 0, 2)
    a_s = pool_sample(zs, hist_t, w_grp, row(pool_scale), ts=ts, bs=bs, past_len=past_len)
    zs3 = zs.reshape(ts, bs, -1)
    kv_rows_s = zs3[:, :, c_kv:c_kv + 4 * KVD].transpose(1, 0, 2).reshape(bs, ts, N_PAGED_SLOTS, N_KV_HEADS, HEAD_DIM)
    win_rows = zs3[:, :, c_kv + 4 * KVD:c_ng].transpose(1, 0, 2).reshape(bs, ts, 2, N_KV_HEADS, HEAD_DIM)
    win_new_s = jnp.concatenate([win_l, win_rows], axis=1)[:, -win_l.shape[1]:]
    pool_new_s = jnp.concatenate([pool_l, zs3[:, :, :pool_w].transpose(1, 0, 2)], axis=1)[:, -POOL_HIST:]

    cache_t = cache_l.transpose(0, 2, 3, 4, 1).reshape(cache_l.shape[0], N_PAGED_SLOTS * KVD, page)
    cmp_s = compress_sample(cache_t, pt_flat, perm, pe_t, w1bd, w2bd, bs=bs, n_pages=n_pages)
    cmp_s = _cmp_lane_layout(cmp_s, 2).astype(BF16)

    rows = GROUP * N_KV_HEADS * ts
    q_s = zs3[:, :, c_q:c_kv].reshape(ts, bs, N_KV_HEADS, GROUP, HEAD_DIM).transpose(1, 3, 2, 0, 4)
    q_bd = jnp.einsum("bgktd,kj->bgktjd", q_s * ATTN_SCALE, jnp.eye(N_KV_HEADS, dtype=F32))
    q_bd = q_bd.reshape(bs, rows, KVD).astype(BF16)
    pad_rows = lambda x: jnp.pad(x.transpose(1, 0, 2), ((0, 0), (0, 16 - ts), (0, 0))).astype(BF16)
    knew = pad_rows(zs3[:, :, c_kv + 2 * KVD:c_kv + 4 * KVD])
    wnew = pad_rows(zs3[:, :, c_kv + 4 * KVD:c_ng])
    wb = win_l.shape[1]
    win_t = win_l.transpose(0, 2, 3, 4, 1).reshape(bs, 2 * KVD, wb)
    r_idx = jnp.arange(rows)
    r_head = ((r_idx % (N_KV_HEADS * ts)) // ts) * GROUP + r_idx // (N_KV_HEADS * ts)
    r_pos = past_len + r_idx % ts
    row_tab = heads.T[r_head]
    rbias = lambda dist: _bias_lookup_rows(row_tab, dist)
    bcmp = rbias(_cmp_dist(r_pos))
    bsel = rbias(r_pos[:, None] - jnp.arange(past_len)[None, :])
    bnew = rbias((r_idx % ts)[:, None] - jnp.arange(16)[None, :])
    bwin = rbias(r_pos[:, None] - (past_len - wb + jnp.arange(wb))[None, :])
    expand = (jnp.arange(LANES)[:, None] == (jnp.arange(past_len) // SEL_BLOCK)[None, :]).astype(BF16)
    ng_rows = ng_s[:, :3 * N_HEADS].reshape(ts, bs, 3, N_KV_HEADS, GROUP).transpose(1, 4, 3, 0, 2).reshape(bs, rows, 3)
    o_s = attn_sample(cache_t, pt_flat, q_bd, knew, win_t, wnew, cmp_s[:, 0], cmp_s[:, 1], bcmp, bsel, bnew, bwin,
                      bnew, expand, ng_rows, bs=bs, ts=ts, n_pages=n_pages, past_len=past_len)
    o_s = o_s.reshape(bs, GROUP, N_KV_HEADS, ts, HEAD_DIM).transpose(3, 0, 2, 1, 4).reshape(n_s, q_w)
    hs_t = merge(a_s, o_s, gates_s, hs_t, wpp, wnp, wo, row(g_post_mix))
    conv_hist = conv_l.transpose(1, 0, 2).reshape((CONV_WIDTH - 1) * bs, d_ff)
    hs_t, tails_s = ffn(hs_t, conv_hist, row(g_pre_ffn), wv, wg, conv_w, row(conv_b), wd, row(g_post_ffn),
                        tm=n_s, tf=512, halo=(CONV_WIDTH - 1) * bs, shift=bs, tiles_per_seq=1,
                        tail=(CONV_WIDTH - 1) * bs)
    conv_new_s = tails_s.reshape(CONV_WIDTH - 1, bs, d_ff).transpose(1, 0, 2)
    return hp, hs_t, (kv_rows_p, kv_rows_s, win_p, win_new_s, pool_new_p, pool_new_s, conv_new_p, conv_new_s)


def kernel(x_prompt, x_sample, cache_kv, page_table, state_kv_win, state_pool, state_conv, g_pre_mix, w_in, pe_cmp_k, w1_cmp_k, w2_cmp_k, pe_cmp_v, w1_cmp_v, w2_cmp_v, rel_bias, w_pool_grp, pool_scale, w_pool_proj, w_nsa_proj, w_out, g_post_mix, g_pre_ffn, w_up, conv_w, conv_b, w_down, g_post_ffn):
    batch, seq, d_model = x_prompt.shape
    bs, ts, _ = x_sample.shape
    depth = cache_kv.shape[0]
    n_pages, page = page_table.shape[1], cache_kv.shape[2]
    assert ts >= CONV_WIDTH - 1
    dims = (batch, seq, bs, ts, n_pages, page)
    pt_flat = page_table.reshape(-1).astype(jnp.int32)
    hp = x_prompt.reshape(batch * seq, d_model)
    hs_t = x_sample.transpose(1, 0, 2).reshape(ts * bs, d_model)
    outs = [[] for _ in range(8)]
    per_layer = (g_pre_mix, w_in, pe_cmp_k, w1_cmp_k, w2_cmp_k, pe_cmp_v, w1_cmp_v, w2_cmp_v, w_pool_grp, pool_scale,
                 w_pool_proj, w_nsa_proj, w_out, g_post_mix, g_pre_ffn, w_up, conv_w, conv_b, w_down, g_post_ffn)
    for l in range(depth):
        wts = tuple(w[l] for w in per_layer)
        hp, hs_t, states = _layer(hp, hs_t, cache_kv[l], pt_flat, state_kv_win[l], state_pool[l], state_conv[l],
                                  wts, rel_bias, dims)
        for acc, s in zip(outs, states):
            acc.append(s)
    y_prompt = hp.reshape(batch, seq, d_model)
    y_sample = hs_t.reshape(ts, bs, d_model).transpose(1, 0, 2)
    return (y_prompt, y_sample) + tuple(jnp.stack(o) for o in outs)
```

```python
import functools
import math

import jax
import jax.numpy as jnp
from jax import lax
from jax.experimental import pallas as pl
from jax.experimental.pallas import tpu as pltpu

F32 = jnp.float32
BF16 = jnp.bfloat16

EPS = 1e-6
POOL_WINDOWS = (2, 4, 8, 16)
POOL_HIST = max(POOL_WINDOWS) - 1
N_HEADS = 16
HEAD_DIM = 64
N_KV_HEADS = 4
GROUP = N_HEADS // N_KV_HEADS
KVD = N_KV_HEADS * HEAD_DIM
N_PAGED_SLOTS = 4
CMP_BLOCK = 32
SEL_BLOCK = 64
SEL_PER = SEL_BLOCK // CMP_BLOCK
N_SELECT = 16
WINDOW = 512
FORCE_BONUS = 1.0e4
ATTN_SCALE = HEAD_DIM ** -0.5
N_BUCKETS = 32
MAX_DISTANCE = 128
CONV_WIDTH = 3
NEG = -1e30
LANES = 128
TQ = 256
SM_LANES = 128
V_ROWS = HEAD_DIM + 16
CMP_GROUP = 4

_NT = (((1,), (1,)), ((), ()))


def _cparams(sem, vmem_mb):
    return pltpu.CompilerParams(dimension_semantics=sem, vmem_limit_bytes=vmem_mb * 1024 * 1024)


def _rms(x, g):
    ms = jnp.mean(x * x, axis=-1, keepdims=True)
    return x * lax.rsqrt(ms + EPS) * g


def _dot(a, b):
    return jnp.dot(a, b, preferred_element_type=F32)


def _dot_nt(a, b):
    return lax.dot_general(a, b, _NT, preferred_element_type=F32)


def _bucket(dist):
    n = jnp.maximum(dist, 0)
    max_exact = N_BUCKETS // 2
    nf = jnp.maximum(n, 1).astype(F32)
    large = max_exact + (jnp.log(nf / max_exact) / math.log(MAX_DISTANCE / max_exact)
                         * (N_BUCKETS - max_exact)).astype(jnp.int32)
    large = jnp.minimum(large, N_BUCKETS - 1)
    return jnp.where(n < max_exact, n, large)


def _bias_lookup(tab, dist):
    bkt = _bucket(dist)[None]
    col = lambda k: tab[:, k].reshape((-1,) + (1,) * dist.ndim)
    out = jnp.broadcast_to(col(N_BUCKETS - 1), (tab.shape[0],) + dist.shape)
    for k in range(N_BUCKETS - 1):
        out = jnp.where(bkt == k, col(k), out)
    return out


def _bias_lookup_rows(tab, dist):
    bkt = _bucket(dist)
    out = jnp.broadcast_to(tab[:, N_BUCKETS - 1:], dist.shape)
    for k in range(N_BUCKETS - 1):
        out = jnp.where(bkt == k, tab[:, k:k + 1], out)
    return out


def _softmax_parts(s, mask):
    s = jnp.where(mask, s, NEG)
    m = jnp.max(s, axis=-1, keepdims=True)
    e = jnp.where(mask, jnp.exp(s - m), 0.0)
    d = jnp.sum(e, axis=-1, keepdims=True)
    return e, d


def _topk_mask(score, n_blocks, k):
    lane = lax.broadcasted_iota(jnp.int32, score.shape, 1)
    rank = jnp.zeros(score.shape, F32)
    for i in range(n_blocks):
        ci = score[:, SEL_PER * i:SEL_PER * i + 1]
        rank = rank + jnp.where(lane > SEL_PER * i, jnp.where(ci >= score, 1.0, 0.0), jnp.where(ci > score, 1.0, 0.0))
    return jnp.where((rank < k) & (lane % SEL_PER == 0) & (lane < SEL_PER * n_blocks), 1.0, 0.0)


def _block_valid(shape, q_pos, n_sb):
    lane = lax.broadcasted_iota(jnp.int32, shape, 1)
    blk = lane // SEL_PER
    return (lane % SEL_PER == 0) & (blk * SEL_BLOCK <= q_pos) & (blk < n_sb)


def _block_scores(p_sum, q_pos, n_sb):
    imp = p_sum
    for k in range(1, SEL_PER):
        imp = imp + pltpu.roll(p_sum, LANES - k, axis=1)
    blk = lax.broadcasted_iota(jnp.int32, imp.shape, 1) // SEL_PER
    cur = q_pos // SEL_BLOCK
    forced = (blk == 0) | (blk == cur) | (blk == cur - 1)
    return jnp.where(_block_valid(imp.shape, q_pos, n_sb), imp + FORCE_BONUS * forced.astype(F32), -jnp.inf)


def _rms_matmul_kernel(x_ref, g_ref, w_ref, o_ref, xn_ref, *, act):
    @pl.when(pl.program_id(1) == 0)
    def _():
        xn_ref[...] = _rms(x_ref[...], g_ref[...]).astype(BF16)

    acc = _dot(xn_ref[...], w_ref[...])
    if act == "sigmoid":
        acc = jax.nn.sigmoid(acc)
    o_ref[...] = acc.astype(o_ref.dtype)


def rms_matmul(x, g, w, *, tm, tn, act=None, name):
    n, k = x.shape
    m = w.shape[1]
    return pl.pallas_call(
        functools.partial(_rms_matmul_kernel, act=act),
        grid=(n // tm, m // tn),
        in_specs=[pl.BlockSpec((tm, k), lambda i, j: (i, 0)),
                  pl.BlockSpec((1, k), lambda i, j: (0, 0)),
                  pl.BlockSpec((k, tn), lambda i, j: (0, j))],
        out_specs=pl.BlockSpec((tm, tn), lambda i, j: (i, j)),
        out_shape=jax.ShapeDtypeStruct((n, m), F32),
        scratch_shapes=[pltpu.VMEM((tm, k), BF16)],
        compiler_params=_cparams(("arbitrary", "arbitrary"), 48),
        name=name,
    )(x, g, w)


def _pool_prompt_kernel(x_ref, h_ref, w_ref, sc_ref, o_ref, buf_ref, *, tt):
    i = pl.program_id(1)
    pg = w_ref.shape[1]
    buf_ref[0:16, :] = jnp.where(i == 0, 0.0, h_ref[...])
    buf_ref[16:, :] = x_ref[...]
    pos = i * tt + lax.broadcasted_iota(jnp.int32, (tt, 1), 0)
    for gi, w in enumerate(POOL_WINDOWS):
        c = slice(gi * pg, (gi + 1) * pg)
        x = buf_ref[16:, c]
        win = x
        for k in range(1, w):
            win = win + buf_ref[16 - k:16 - k + tt, c]
        cnt = jnp.minimum(w, pos + 1).astype(F32)
        pooled = win / cnt - x
        o_ref[:, c] = (_dot(pooled.astype(BF16), w_ref[gi]) * sc_ref[:, c]).astype(o_ref.dtype)


def pool_prompt(z, w_grp, scale, *, batch, seq, tt):
    pw = w_grp.shape[0] * w_grp.shape[1]
    nt = seq // tt
    return pl.pallas_call(
        functools.partial(_pool_prompt_kernel, tt=tt),
        grid=(batch, nt),
        in_specs=[pl.BlockSpec((tt, pw), lambda b, i: (b * nt + i, 0)),
                  pl.BlockSpec((16, pw), lambda b, i: (jnp.maximum((b * nt + i) * (tt // 16) - 1, 0), 0)),
                  pl.BlockSpec(w_grp.shape, lambda b, i: (0, 0, 0)),
                  pl.BlockSpec((1, pw), lambda b, i: (0, 0))],
        out_specs=pl.BlockSpec((tt, pw), lambda b, i: (b * nt + i, 0)),
        out_shape=jax.ShapeDtypeStruct((batch * seq, pw), BF16),
        scratch_shapes=[pltpu.VMEM((tt + 16, pw), F32)],
        compiler_params=_cparams(("arbitrary", "arbitrary"), 32),
        name="pool_prompt",
    )(z, z, w_grp, scale)


def _pool_sample_kernel(x_ref, h_ref, w_ref, sc_ref, o_ref, *, ts, bs, past_len):
    pg = w_ref.shape[1]

    def row(j, c):
        if j < POOL_HIST:
            return h_ref[j, :, c]
        return x_ref[(j - POOL_HIST) * bs:(j - POOL_HIST + 1) * bs, c]

    for t in range(ts):
        for gi, w in enumerate(POOL_WINDOWS):
            c = slice(gi * pg, (gi + 1) * pg)
            x = row(POOL_HIST + t, c)
            win = x
            for k in range(1, w):
                win = win + row(POOL_HIST + t - k, c)
            cnt = float(min(w, past_len + t + 1))
            pooled = win / cnt - x
            o_ref[t * bs:(t + 1) * bs, c] = (_dot(pooled.astype(BF16), w_ref[gi]) * sc_ref[:, c]).astype(o_ref.dtype)


def pool_sample(z, hist_t, w_grp, scale, *, ts, bs, past_len):
    pw = w_grp.shape[0] * w_grp.shape[1]
    return pl.pallas_call(
        functools.partial(_pool_sample_kernel, ts=ts, bs=bs, past_len=past_len),
        grid=(1,),
        in_specs=[pl.BlockSpec((ts * bs, pw), lambda i: (0, 0)),
                  pl.BlockSpec(hist_t.shape, lambda i: (0, 0, 0)),
                  pl.BlockSpec(w_grp.shape, lambda i: (0, 0, 0)),
                  pl.BlockSpec((1, pw), lambda i: (0, 0))],
        out_specs=pl.BlockSpec((ts * bs, pw), lambda i: (0, 0)),
        out_shape=jax.ShapeDtypeStruct((ts * bs, pw), BF16),
        compiler_params=_cparams(("arbitrary",), 48),
        name="pool_sample",
    )(z, hist_t, w_grp, scale)


def _compress_tail(hid, w2_ref):
    hid = jax.nn.gelu(hid, approximate=True)
    return _dot(hid.astype(BF16), w2_ref[...])


def _compress_prompt_kernel(x_ref, perm_ref, pe_ref, w1_ref, w2_ref, o_ref, xs_ref, *, n_blk, chunk):
    n_chunks = x_ref.shape[0] // chunk
    blk_chunk = chunk // CMP_BLOCK
    for s in range(2):
        c = slice(s * KVD, (s + 1) * KVD)
        for ch in range(n_chunks):
            x = (x_ref[ch * chunk:(ch + 1) * chunk, c] + pe_ref[s]).astype(BF16)
            xs_ref[ch] = _dot(perm_ref[...], x).reshape(CMP_BLOCK, blk_chunk, KVD)
        hid = jnp.zeros((n_blk, KVD), F32)
        for l in range(CMP_BLOCK):
            hid = hid + _dot(xs_ref[:, l].reshape(n_blk, KVD).astype(BF16), w1_ref[s, l])
        o_ref[0, s] = _compress_tail(hid, w2_ref.at[s])


def compress_prompt(z, col_block, perm, pe_n, w1bd, w2bd, *, batch, seq):
    n_blk = seq // CMP_BLOCK
    chunk = perm.shape[0]
    assert seq % chunk == 0
    return pl.pallas_call(
        functools.partial(_compress_prompt_kernel, n_blk=n_blk, chunk=chunk),
        grid=(batch,),
        in_specs=[pl.BlockSpec((seq, 2 * KVD), lambda b: (b, col_block)),
                  pl.BlockSpec(perm.shape, lambda b: (0, 0)),
                  pl.BlockSpec(pe_n.shape, lambda b: (0, 0, 0)),
                  pl.BlockSpec(w1bd.shape, lambda b: (0, 0, 0, 0), pipeline_mode=pl.Buffered(1)),
                  pl.BlockSpec(w2bd.shape, lambda b: (0, 0, 0))],
        out_specs=pl.BlockSpec((1, 2, n_blk, KVD), lambda b: (b, 0, 0, 0)),
        out_shape=jax.ShapeDtypeStruct((batch, 2, n_blk, KVD), F32),
        scratch_shapes=[pltpu.VMEM((seq // chunk, CMP_BLOCK, chunk // CMP_BLOCK, KVD), F32)],
        compiler_params=_cparams(("arbitrary",), 48),
        name="compress_prompt",
    )(z, perm, pe_n, w1bd, w2bd)


def _compress_sample_kernel(pt_ref, *refs, n_pages, page):
    del pt_ref
    page_refs = refs[:n_pages]
    perm_ref, pe_ref, w1_ref, w2_ref, o_ref, x_ref = refs[n_pages:]
    b = pl.program_id(0)
    bl = b % CMP_GROUP
    n_pairs = n_pages // 2
    blk_pair = 2 * page // CMP_BLOCK
    for pp in range(n_pairs):
        for s in range(2):
            r = slice(s * KVD, (s + 1) * KVD)
            xt = jnp.concatenate([page_refs[2 * pp][0, r, :], page_refs[2 * pp + 1][0, r, :]], axis=1)
            xt = (xt + pe_ref[s]).astype(BF16)
            xp = _dot_nt(perm_ref[...], xt)
            x_ref[s, bl * n_pairs + pp] = xp.reshape(CMP_BLOCK, blk_pair, KVD)

    @pl.when(bl == CMP_GROUP - 1)
    def _():
        rows = CMP_GROUP * n_pairs * blk_pair
        for s in range(2):
            hid = jnp.zeros((rows, KVD), F32)
            for l in range(CMP_BLOCK):
                xl = x_ref[s, :, l].reshape(rows, KVD)
                hid = hid + _dot(xl.astype(BF16), w1_ref[s, l])
            o_ref[:, s] = _compress_tail(hid, w2_ref.at[s]).reshape(CMP_GROUP, n_pairs * blk_pair, KVD)


def compress_sample(cache_t, pt_flat, perm, pe_t, w1bd, w2bd, *, bs, n_pages):
    page = cache_t.shape[2]
    assert page == LANES and n_pages % 2 == 0 and bs % CMP_GROUP == 0
    n_pairs = n_pages // 2
    blk_pair = 2 * page // CMP_BLOCK
    n_blk = n_pairs * blk_pair

    def page_map(b, pt, *, p):
        return (pt[b * n_pages + p], 0, 0)

    const = lambda nd: (lambda b, pt: (0,) * nd)
    grid_spec = pltpu.PrefetchScalarGridSpec(
        num_scalar_prefetch=1,
        grid=(bs,),
        in_specs=[pl.BlockSpec((1, 2 * KVD, page), functools.partial(page_map, p=p)) for p in range(n_pages)]
        + [pl.BlockSpec(perm.shape, const(2)), pl.BlockSpec(pe_t.shape, const(3)),
           pl.BlockSpec(w1bd.shape, const(4), pipeline_mode=pl.Buffered(1)), pl.BlockSpec(w2bd.shape, const(3))],
        out_specs=pl.BlockSpec((CMP_GROUP, 2, n_blk, KVD), lambda b, pt: (b // CMP_GROUP, 0, 0, 0)),
        scratch_shapes=[pltpu.VMEM((2, CMP_GROUP * n_pairs, CMP_BLOCK, blk_pair, KVD), F32)],
    )
    return pl.pallas_call(
        functools.partial(_compress_sample_kernel, n_pages=n_pages, page=page),
        grid_spec=grid_spec,
        out_shape=jax.ShapeDtypeStruct((bs, 2, n_blk, KVD), F32),
        compiler_params=_cparams(("arbitrary",), 56),
        name="compress_sample",
    )(pt_flat, *([cache_t] * n_pages), perm, pe_t, w1bd, w2bd)


def _cmp_valid(q_pos, shape, n_cmp):
    n = lax.broadcasted_iota(jnp.int32, shape, 1) % LANES
    return (n < n_cmp) & (n * CMP_BLOCK + CMP_BLOCK - 1 <= q_pos)


def _cmp_select_prompt_kernel(q_ref, kc_ref, vc_ref, bias_ref, ocmp_ref, sel_ref, *, n_cmp, n_sb):
    qb = pl.program_id(2)
    q_pos = qb * TQ + lax.broadcasted_iota(jnp.int32, (TQ, 1), 0)
    valid = _cmp_valid(q_pos, (TQ, LANES), n_cmp)
    q = (q_ref[...] * ATTN_SCALE).astype(BF16)
    s = _dot(q, kc_ref[0, 0]) + bias_ref[0]
    p_sum = jnp.zeros((TQ, LANES), F32)
    probs = []
    for g in range(GROUP):
        e, d = _softmax_parts(s[:, g * LANES:(g + 1) * LANES], valid)
        p = e / jnp.where(d > 0, d, 1.0)
        probs.append(p.astype(BF16))
        p_sum = p_sum + p
    ocmp_ref[...] = _dot(jnp.concatenate(probs, axis=1), vc_ref[0, 0])

    k_sel = min(N_SELECT, n_sb)
    all_fit = (qb + 1) * TQ <= k_sel * SEL_BLOCK

    @pl.when(all_fit)
    def _():
        sel_ref[0, 0] = jnp.where(_block_valid((TQ, LANES), q_pos, n_sb), 1.0, 0.0)

    @pl.when(jnp.logical_not(all_fit))
    def _():
        sel_ref[0, 0] = _topk_mask(_block_scores(p_sum, q_pos, n_sb), n_sb, k_sel)


def cmp_select_prompt(z, q_col, kc_bd, vc_bd, bias_cmp, *, batch, seq):
    n_cmp = seq // CMP_BLOCK
    n_sb = -(-seq // SEL_BLOCK)
    nq = seq // TQ
    gw = GROUP * HEAD_DIM
    assert SEL_PER * n_sb <= LANES and n_cmp <= LANES
    return pl.pallas_call(
        functools.partial(_cmp_select_prompt_kernel, n_cmp=n_cmp, n_sb=n_sb),
        grid=(batch, N_KV_HEADS, nq),
        in_specs=[pl.BlockSpec((TQ, gw), lambda b, kv, i: (b * nq + i, q_col + kv)),
                  pl.BlockSpec((1, 1, gw, GROUP * LANES), lambda b, kv, i: (b, kv, 0, 0)),
                  pl.BlockSpec((1, 1, GROUP * LANES, gw), lambda b, kv, i: (b, kv, 0, 0)),
                  pl.BlockSpec((1, TQ, GROUP * LANES), lambda b, kv, i: (kv, i, 0))],
        out_specs=[pl.BlockSpec((TQ, gw), lambda b, kv, i: (b * nq + i, kv)),
                   pl.BlockSpec((1, 1, TQ, LANES), lambda b, kv, i: (b, kv, i, 0))],
        out_shape=[jax.ShapeDtypeStruct((batch * seq, N_HEADS * HEAD_DIM), F32),
                   jax.ShapeDtypeStruct((batch, N_KV_HEADS, seq, LANES), F32)],
        compiler_params=_cparams(("arbitrary",) * 3, 32),
        name="cmp_select_prompt",
    )(z, kc_bd, vc_bd, bias_cmp)


def _attn_prompt_kernel(far_ref, q_ref, ks_ref, vs_ref, kw_ref, vw_ref, sel_ref, ocmp_ref, ng_ref, bt_ref, o_ref,
                        qbd_ref, s_ref, p_ref, term_ref, m_ref, a_ref, acc_ref, osel_ref, ngt_ref):
    kv = pl.program_id(1)
    qb = pl.program_id(2)
    gw = GROUP * HEAD_DIM
    selm = sel_ref[0, 0].astype(BF16)
    tok_minus_key = lax.broadcasted_iota(jnp.int32, (TQ, TQ), 1) - lax.broadcasted_iota(jnp.int32, (TQ, TQ), 0)
    blk_of_key = lax.broadcasted_iota(jnp.int32, (TQ, LANES), 0) // SEL_BLOCK
    blk_lane = lax.broadcasted_iota(jnp.int32, (TQ, LANES), 1)
    near0 = jnp.maximum(qb - 1, 0)
    eye = (lax.broadcasted_iota(jnp.int32, (gw, gw), 0) == lax.broadcasted_iota(jnp.int32, (gw, gw), 1)).astype(BF16)
    qt = _dot_nt(eye, (q_ref[...] * ATTN_SCALE).astype(BF16)).astype(BF16)
    qbd_ref[...] = jnp.zeros(qbd_ref.shape, BF16)
    row0 = pl.multiple_of(kv * HEAD_DIM, HEAD_DIM)
    for g in range(GROUP):
        qbd_ref[pl.ds(row0, HEAD_DIM), g * TQ:(g + 1) * TQ] = qt[g * HEAD_DIM:(g + 1) * HEAD_DIM, :]
    vrow = lax.broadcasted_iota(jnp.int32, (V_ROWS, KVD), 0)
    vpick = ((lax.broadcasted_iota(jnp.int32, (V_ROWS, KVD), 1) == kv * HEAD_DIM + vrow) & (vrow < HEAD_DIM))
    vpick = vpick.astype(BF16)
    ones_row = lax.broadcasted_iota(jnp.int32, (V_ROWS, TQ), 0) == HEAD_DIM

    def chunk(j, k_ref, v_ref, term, near):
        start = pl.multiple_of(j * TQ, TQ)
        s_ref[...] = _dot(k_ref[pl.ds(start, TQ), :].astype(BF16), qbd_ref[...])
        vt = jnp.where(ones_row, 1.0, _dot_nt(vpick, v_ref[pl.ds(start, TQ), :].astype(BF16))).astype(BF16)
        term_ref[...] = term
        for g in range(GROUP):
            far = far_ref[kv * GROUP + g]
            for half in range(TQ // SM_LANES):
                c = slice(g * TQ + half * SM_LANES, g * TQ + (half + 1) * SM_LANES)
                t = slice(half * SM_LANES, (half + 1) * SM_LANES)
                u = s_ref[:, c] + term_ref[:, t]
                if near:
                    u = u + bt_ref[g, j - (qb - 1), :, t]
                m_old = m_ref[:, c]
                mx = jnp.max(u, axis=0, keepdims=True)
                m_new = jnp.maximum(m_old, mx if near else mx + far)
                a_ref[:, c] = jnp.exp(m_old - m_new)
                m_ref[:, c] = m_new
                p_ref[:, c] = jnp.exp(u - (m_new if near else m_new - far)).astype(BF16)
        acc_ref[...] = a_ref[...] * acc_ref[...] + _dot(vt, p_ref[...])

    def sel_term(j):
        expand = (blk_lane == SEL_PER * (blk_of_key + j * (TQ // SEL_BLOCK))).astype(BF16)
        return (_dot_nt(expand, selm) - 1.0) * (-NEG)

    def causal_term(j):
        return jnp.where(tok_minus_key + (qb - j) * TQ >= 0, 0.0, NEG)

    def run(k_ref, vt_ref, selected):
        m_ref[...] = jnp.full(m_ref.shape, NEG, F32)
        acc_ref[...] = jnp.zeros(acc_ref.shape, F32)
        if selected:
            def far_body(j, c):
                chunk(j, k_ref, vt_ref, sel_term(j), False)
                return c
            lax.fori_loop(0, near0, far_body, 0)
        else:
            @pl.when(qb >= WINDOW // TQ)
            def _():
                chunk(qb - WINDOW // TQ, k_ref, vt_ref, jnp.where(tok_minus_key <= 0, 0.0, NEG), False)

        def near_body(j, c):
            term = causal_term(j) + sel_term(j) if selected else causal_term(j)
            chunk(j, k_ref, vt_ref, term, True)
            return c
        lax.fori_loop(near0, qb + 1, near_body, 0)
        l = acc_ref[HEAD_DIM:HEAD_DIM + 1, :]
        return acc_ref[0:HEAD_DIM, :] / jnp.where(l > 0, l, 1.0)

    osel_ref[...] = run(ks_ref, vs_ref, True)
    o_win = run(kw_ref, vw_ref, False)
    ngt_ref[...] = ng_ref[...].T
    ocmp_t = ocmp_ref[...].T
    outs = []
    for g in range(GROUP):
        gate = lambda r: ngt_ref[pl.ds(r * N_HEADS + kv * GROUP + g, 1), :]
        c = slice(g * TQ, (g + 1) * TQ)
        outs.append(gate(0) * ocmp_t[g * HEAD_DIM:(g + 1) * HEAD_DIM, :] + gate(1) * osel_ref[:, c]
                    + gate(2) * o_win[:, c])
    o_t = jnp.concatenate(outs, axis=0).astype(BF16)
    o_ref[...] = _dot_nt(eye, o_t).astype(o_ref.dtype)


def attn_prompt(far, z, q_col, kv_col, sel, ocmp, ng, bias_tile, *, batch, seq):
    assert WINDOW == 2 * TQ and seq % TQ == 0 and TQ % SM_LANES == 0 and TQ == GROUP * HEAD_DIM == KVD
    cols = GROUP * TQ
    nq = seq // TQ
    kv_spec = lambda slot: pl.BlockSpec((seq, KVD), lambda b, kv, i: (b, kv_col + slot))
    tok_spec = lambda col0: pl.BlockSpec((TQ, KVD), lambda b, kv, i: (b * nq + i, col0 + kv))
    return pl.pallas_call(
        _attn_prompt_kernel,
        grid=(batch, N_KV_HEADS, nq),
        in_specs=[pl.BlockSpec(memory_space=pltpu.SMEM),
                  tok_spec(q_col), kv_spec(0), kv_spec(1), kv_spec(2), kv_spec(3),
                  pl.BlockSpec((1, 1, TQ, LANES), lambda b, kv, i: (b, kv, i, 0)),
                  tok_spec(0),
                  pl.BlockSpec((TQ, LANES), lambda b, kv, i: (b * nq + i, 0)),
                  pl.BlockSpec((GROUP, 2, TQ, TQ), lambda b, kv, i: (kv, 0, 0, 0))],
        out_specs=tok_spec(0),
        out_shape=jax.ShapeDtypeStruct((batch * seq, N_HEADS * HEAD_DIM), BF16),
        scratch_shapes=[pltpu.VMEM((KVD, cols), BF16),
                        pltpu.VMEM((TQ, cols), F32), pltpu.VMEM((TQ, cols), BF16), pltpu.VMEM((TQ, TQ), F32),
                        pltpu.VMEM((1, cols), F32), pltpu.VMEM((1, cols), F32),
                        pltpu.VMEM((V_ROWS, cols), F32), pltpu.VMEM((HEAD_DIM, cols), F32),
                        pltpu.VMEM((LANES, TQ), F32)],
        compiler_params=_cparams(("arbitrary",) * 3, 48),
        name="attn_prompt",
    )(far, z, z, z, z, z, sel, ocmp, ng, bias_tile)


def _diag_heads(x, ts):
    rows = x.shape[0]
    kv_of_row = (lax.broadcasted_iota(jnp.int32, (rows, HEAD_DIM), 0) % (N_KV_HEADS * ts)) // ts
    out = jnp.zeros((rows, HEAD_DIM), F32)
    for kv in range(N_KV_HEADS):
        out = out + jnp.where(kv_of_row == kv, x[:, kv * HEAD_DIM:(kv + 1) * HEAD_DIM], 0.0)
    return out


def _attn_sample_kernel(pt_ref, *refs, n_pages, page, ts, past_len, wb):
    del pt_ref
    page_refs = refs[:n_pages]
    (q_ref, knew_ref, win_ref, wnew_ref, kc_ref, vc_ref, bcmp_ref, bsel_ref, bseln_ref, bwin_ref, bwinn_ref,
     expand_ref, ng_ref, o_ref, kt_ref, vt_ref) = refs[n_pages:]
    rows = GROUP * N_KV_HEADS * ts
    kvt = N_KV_HEADS * ts
    n_cmp = past_len // CMP_BLOCK
    n_sb = -(-(past_len + ts) // SEL_BLOCK)
    q = q_ref[0]
    t_of_row = lax.broadcasted_iota(jnp.int32, (rows, 1), 0) % ts
    q_pos = past_len + t_of_row

    s = _dot_nt(q, kc_ref[0]) + bcmp_ref[...]
    e, d = _softmax_parts(s, _cmp_valid(q_pos, (rows, LANES), n_cmp))
    p = e / jnp.where(d > 0, d, 1.0)
    o_cmp = _diag_heads(_dot(p.astype(BF16), vc_ref[0]), ts)
    p_sum = p[0:kvt]
    for g in range(1, GROUP):
        p_sum = p_sum + p[g * kvt:(g + 1) * kvt]
    sel = _topk_mask(_block_scores(p_sum, q_pos[0:kvt], n_sb), n_sb, min(N_SELECT, n_sb))
    sel = jnp.concatenate([sel] * GROUP, axis=0)

    for pg in range(n_pages):
        kt_ref[:, pg * page:(pg + 1) * page] = page_refs[pg][0, 0:KVD, :].astype(BF16)
        vt_ref[:, pg * page:(pg + 1) * page] = page_refs[pg][0, KVD:2 * KVD, :].astype(BF16)
    new_col = lax.broadcasted_iota(jnp.int32, (rows, 16), 1)
    new_ok = (new_col <= t_of_row) & (new_col < ts)
    s = _dot(q, kt_ref[...]) + bsel_ref[...]
    mask = _dot(sel.astype(BF16), expand_ref[...]) > 0.5
    sn = _dot_nt(q, knew_ref[0, :, 0:KVD]) + bseln_ref[...]
    new_lane = SEL_PER * (past_len // SEL_BLOCK)
    mask_n = new_ok & (sel[:, new_lane:new_lane + 1] > 0.5)
    o_sel = _two_part_attention(s, mask, vt_ref[...], sn, mask_n, knew_ref[0, :, KVD:2 * KVD], ts)

    s = _dot(q, win_ref[0, 0:KVD, :].astype(BF16)) + bwin_ref[...]
    dist = wb + t_of_row - lax.broadcasted_iota(jnp.int32, (rows, wb), 1)
    mask = (dist >= 0) & (dist <= WINDOW)
    sn = _dot_nt(q, wnew_ref[0, :, 0:KVD]) + bwinn_ref[...]
    o_win = _two_part_attention(s, mask, win_ref[0, KVD:2 * KVD, :].astype(BF16), sn, new_ok,
                                wnew_ref[0, :, KVD:2 * KVD], ts)

    ng = ng_ref[0]
    o_ref[0] = (ng[:, 0:1] * o_cmp + ng[:, 1:2] * o_sel + ng[:, 2:3] * o_win).astype(o_ref.dtype)


def _two_part_attention(s, mask, vt, sn, mask_n, vn, ts):
    s = jnp.where(mask, s, NEG)
    sn = jnp.where(mask_n, sn, NEG)
    m = jnp.maximum(jnp.max(s, axis=-1, keepdims=True), jnp.max(sn, axis=-1, keepdims=True))
    e = jnp.where(mask, jnp.exp(s - m), 0.0)
    en = jnp.where(mask_n, jnp.exp(sn - m), 0.0)
    d = jnp.sum(e, axis=-1, keepdims=True) + jnp.sum(en, axis=-1, keepdims=True)
    inv = jnp.where(d > 0, d, 1.0)
    o = _dot_nt((e / inv).astype(BF16), vt) + _dot((en / inv).astype(BF16), vn)
    return _diag_heads(o, ts)


def attn_sample(cache_t, pt_flat, q_bd, knew, win_t, wnew, kc_s, vc_s, bcmp, bsel, bseln, bwin, bwinn, expand, ng_s,
                *, bs, ts, n_pages, past_len):
    page = cache_t.shape[2]
    wb = win_t.shape[2]
    rows = GROUP * N_KV_HEADS * ts
    assert past_len == n_pages * page and past_len % SEL_BLOCK == 0 and ts <= 16
    assert (past_len + ts) // CMP_BLOCK == past_len // CMP_BLOCK

    def page_map(b, pt, *, p):
        return (pt[b * n_pages + p], 1, 0)

    per_b = lambda shape: pl.BlockSpec((1,) + shape, lambda b, pt: (b,) + (0,) * len(shape))
    const = lambda a: pl.BlockSpec(a.shape, lambda b, pt: (0,) * a.ndim)
    grid_spec = pltpu.PrefetchScalarGridSpec(
        num_scalar_prefetch=1,
        grid=(bs,),
        in_specs=[pl.BlockSpec((1, 2 * KVD, page), functools.partial(page_map, p=p)) for p in range(n_pages)]
        + [per_b((rows, KVD)), per_b((16, 2 * KVD)), per_b((2 * KVD, wb)), per_b((16, 2 * KVD)),
           per_b((LANES, KVD)), per_b((LANES, KVD)),
           const(bcmp), const(bsel), const(bseln), const(bwin), const(bwinn), const(expand),
           per_b((rows, 3))],
        out_specs=per_b((rows, HEAD_DIM)),
        scratch_shapes=[pltpu.VMEM((KVD, past_len), BF16), pltpu.VMEM((KVD, past_len), BF16)],
    )
    return pl.pallas_call(
        functools.partial(_attn_sample_kernel, n_pages=n_pages, page=page, ts=ts, past_len=past_len, wb=wb),
        grid_spec=grid_spec,
        out_shape=jax.ShapeDtypeStruct((bs, rows, HEAD_DIM), BF16),
        compiler_params=_cparams(("arbitrary",), 48),
        name="attn_sample",
    )(pt_flat, *([cache_t] * n_pages), q_bd, knew, win_t, wnew, kc_s, vc_s, bcmp, bsel, bseln, bwin, bwinn, expand,
      ng_s)


def _merge_kernel(a_ref, o_ref, g0_ref, g1_ref, h_ref, wp_ref, wn_ref, wo_ref, gp_ref, out_ref):
    merged = g0_ref[...] * _dot(a_ref[...], wp_ref[...]) + g1_ref[...] * _dot(o_ref[...], wn_ref[...])
    r = _dot(merged.astype(BF16), wo_ref[...])
    out_ref[...] = h_ref[...] + _rms(r, gp_ref[...])


def merge(a, o, gates, h, wp, wn, wo, gp):
    n, d = h.shape
    tm = min(256, n)
    pw = a.shape[1]
    qw = o.shape[1]
    once = pl.Buffered(1)
    return pl.pallas_call(
        _merge_kernel,
        grid=(n // tm,),
        in_specs=[pl.BlockSpec((tm, pw), lambda i: (i, 0)),
                  pl.BlockSpec((tm, qw), lambda i: (i, 0)),
                  pl.BlockSpec((tm, d), lambda i: (i, 0)),
                  pl.BlockSpec((tm, d), lambda i: (i, 1)),
                  pl.BlockSpec((tm, d), lambda i: (i, 0)),
                  pl.BlockSpec(wp.shape, lambda i: (0, 0), pipeline_mode=once),
                  pl.BlockSpec(wn.shape, lambda i: (0, 0), pipeline_mode=once),
                  pl.BlockSpec(wo.shape, lambda i: (0, 0), pipeline_mode=once),
                  pl.BlockSpec((1, d), lambda i: (0, 0))],
        out_specs=pl.BlockSpec((tm, d), lambda i: (i, 0)),
        out_shape=jax.ShapeDtypeStruct((n, d), F32),
        compiler_params=_cparams(("arbitrary",), 56),
        name="merge",
    )(a, o, gates, gates, h, wp, wn, wo, gp)


def _ffn_kernel(h_ref, halo_ref, gpre_ref, wv_ref, wg_ref, cw_ref, cb_ref, wd_ref, gpost_ref,
                out_ref, tail_ref, xn_ref, acc_ref, gbuf_ref, *, tm, halo, shift, tiles_per_seq, tail, from_rows):
    i = pl.program_id(0)
    f = pl.program_id(1)

    @pl.when(f == 0)
    def _():
        xn_ref[halo:, :] = _rms(h_ref[...], gpre_ref[...]).astype(BF16)
        if from_rows:
            xn_ref[0:halo, :] = _rms(halo_ref[...], gpre_ref[...]).astype(BF16)
        acc_ref[...] = jnp.zeros_like(acc_ref)

    val = _dot(xn_ref[halo:, :], wv_ref[...])
    if from_rows:
        gate_all = _dot(xn_ref[...], wg_ref[...])
        gbuf_ref[0:halo, :] = jnp.where(i % tiles_per_seq == 0, 0.0, gate_all[0:halo])
        gate = gate_all[halo:]
    else:
        gbuf_ref[0:halo, :] = halo_ref[...]
        gate = _dot(xn_ref[halo:, :], wg_ref[...])
    gbuf_ref[halo:, :] = gate
    conv = cb_ref[...]
    for j in range(CONV_WIDTH):
        back = (CONV_WIDTH - 1 - j) * shift
        conv = conv + cw_ref[j:j + 1, :] * gbuf_ref[halo - back:halo - back + tm, :]
    act = jax.nn.gelu(conv, approximate=True) * val
    acc_ref[...] += _dot(act.astype(BF16), wd_ref[...])
    tail_ref[...] = gate[tm - tail:, :]

    @pl.when(f == pl.num_programs(1) - 1)
    def _():
        out_ref[...] = h_ref[...] + _rms(acc_ref[...], gpost_ref[...])


def ffn(h, halo_src, gpre, w_up, cw, cb, wd, gpost, *, tm, tf, halo, shift, tiles_per_seq, tail):
    n, d = h.shape
    dff = wd.shape[0]
    nt = n // tm
    assert dff % tf == 0
    from_rows = halo_src is None
    if from_rows:
        halo_src = h
        halo_spec = pl.BlockSpec((halo, d), lambda i, f: (jnp.maximum(i * (tm // halo) - 1, 0), 0))
    else:
        assert nt == 1
        halo_spec = pl.BlockSpec((halo, tf), lambda i, f: (0, f))
    return pl.pallas_call(
        functools.partial(_ffn_kernel, tm=tm, halo=halo, shift=shift, tiles_per_seq=tiles_per_seq, tail=tail,
                          from_rows=from_rows),
        grid=(nt, dff // tf),
        in_specs=[pl.BlockSpec((tm, d), lambda i, f: (i, 0)),
                  halo_spec,
                  pl.BlockSpec((1, d), lambda i, f: (0, 0)),
                  pl.BlockSpec((d, tf), lambda i, f: (0, f)),
                  pl.BlockSpec((d, tf), lambda i, f: (0, dff // tf + f)),
                  pl.BlockSpec((CONV_WIDTH, tf), lambda i, f: (0, f)),
                  pl.BlockSpec((1, tf), lambda i, f: (0, f)),
                  pl.BlockSpec((tf, d), lambda i, f: (f, 0)),
                  pl.BlockSpec((1, d), lambda i, f: (0, 0))],
        out_specs=[pl.BlockSpec((tm, d), lambda i, f: (i, 0)),
                   pl.BlockSpec((tail, tf), lambda i, f: (i, f))],
        out_shape=[jax.ShapeDtypeStruct((n, d), F32),
                   jax.ShapeDtypeStruct((nt * tail, dff), F32)],
        scratch_shapes=[pltpu.VMEM((tm + halo, d), BF16), pltpu.VMEM((tm, d), F32),
                        pltpu.VMEM((tm + halo, tf), F32)],
        compiler_params=_cparams(("arbitrary", "arbitrary"), 56),
        name="ffn",
    )(h, halo_src, gpre, w_up, w_up, cw, cb, wd, gpost)


def _block_diag4(w):
    eye = jnp.eye(N_KV_HEADS, dtype=w.dtype)
    out = jnp.einsum("gh,...ab->...gahb", eye, w)
    return out.reshape(w.shape[:-2] + (N_KV_HEADS * w.shape[-2], N_KV_HEADS * w.shape[-1]))


def _cmp_dist(q_pos):
    return q_pos[:, None] - (jnp.arange(LANES) * CMP_BLOCK + CMP_BLOCK - 1)[None, :]


def _layer(hp, hs_t, cache_l, pt_flat, win_l, pool_l, conv_l, wts, table, dims):
    (g_pre_mix, w_in, pe_k, w1_k, w2_k, pe_v, w1_v, w2_v, w_pool_grp, pool_scale, w_pool_proj, w_nsa_proj,
     w_out, g_post_mix, g_pre_ffn, w_up, conv_w, conv_b, w_down, g_post_ffn) = wts
    batch, seq, bs, ts, n_pages, page = dims
    d_model = hp.shape[1]
    pool_w = w_pool_grp.shape[0] * w_pool_grp.shape[1]
    q_w = N_HEADS * HEAD_DIM
    d_ff = w_down.shape[0]
    past_len = n_pages * page
    c_q, c_kv, c_ng, c_mg = pool_w, pool_w + q_w, pool_w + q_w + 6 * KVD, pool_w + q_w + 6 * KVD + 3 * N_HEADS
    assert pool_w == q_w == 4 * KVD, "column blocks below assume equal widths"

    row = lambda v: v.reshape(1, -1)
    w_a = w_in[:, :c_ng].astype(BF16)
    w_ng = jnp.pad(w_in[:, c_ng:c_mg], ((0, 0), (0, LANES - 3 * N_HEADS))).astype(BF16)
    w_mg = w_in[:, c_mg:].astype(BF16)
    w_grp = w_pool_grp.astype(BF16)
    w1bd = jnp.stack([_block_diag4(w1_k), _block_diag4(w1_v)]).astype(BF16)
    w2bd = jnp.stack([_block_diag4(w2_k), _block_diag4(w2_v)]).astype(BF16)
    chunk = 2 * page
    blk_chunk = chunk // CMP_BLOCK
    regroup = (jnp.arange(chunk) % blk_chunk) * CMP_BLOCK + jnp.arange(chunk) // blk_chunk
    perm = (jnp.arange(chunk)[None, :] == regroup[:, None]).astype(BF16)
    pe_n = jnp.stack([jnp.tile(pe, (blk_chunk, N_KV_HEADS)) for pe in (pe_k, pe_v)])
    pe_t = pe_n.transpose(0, 2, 1)
    w_up16 = w_up.astype(BF16)
    wd = w_down.astype(BF16)
    wpp, wnp, wo = w_pool_proj.astype(BF16), w_nsa_proj.astype(BF16), w_out.astype(BF16)
    heads = table.reshape(N_BUCKETS, N_HEADS)

    z = rms_matmul(hp, row(g_pre_mix), w_a, tm=1024, tn=512, name="proj_main_prompt")
    gates = rms_matmul(hp, row(g_pre_mix), w_mg, tm=1024, tn=512, act="sigmoid", name="proj_gate_prompt")
    ng = rms_matmul(hp, row(g_pre_mix), w_ng, tm=1024, tn=LANES, act="sigmoid", name="proj_ng_prompt")
    a_p = pool_prompt(z, w_grp, row(pool_scale), batch=batch, seq=seq, tt=512)
    z3 = z.reshape(batch, seq, -1)
    kv_rows_p = z3[:, :, c_kv:c_kv + 4 * KVD].reshape(batch, seq, N_PAGED_SLOTS, N_KV_HEADS, HEAD_DIM)
    win_p = z3[:, seq - min(WINDOW, seq):, c_kv + 4 * KVD:c_ng].reshape(batch, -1, 2, N_KV_HEADS, HEAD_DIM)
    pool_new_p = z3[:, seq - POOL_HIST:, :pool_w]

    cmp_p = compress_prompt(z, c_kv // (2 * KVD), perm, pe_n, w1bd, w2bd, batch=batch, seq=seq)
    cmp_p = jnp.pad(cmp_p, ((0, 0), (0, 0), (0, LANES - cmp_p.shape[2]), (0, 0)))
    cmp_p = cmp_p.reshape(batch, 2, LANES, N_KV_HEADS, HEAD_DIM).astype(BF16)
    eye_g = jnp.eye(GROUP, dtype=BF16)
    kc_bd = jnp.einsum("bnkd,gh->bkgdhn", cmp_p[:, 0], eye_g).reshape(batch, N_KV_HEADS, KVD, GROUP * LANES)
    vc_bd = jnp.einsum("bnkd,gh->bkgnhd", cmp_p[:, 1], eye_g).reshape(batch, N_KV_HEADS, GROUP * LANES, KVD)
    pos_p = jnp.arange(seq)
    bias_cmp = _bias_lookup(heads.T, _cmp_dist(pos_p))
    bias_cmp = (bias_cmp.reshape(N_KV_HEADS, GROUP, seq, LANES).transpose(0, 2, 1, 3)
                .reshape(N_KV_HEADS, seq, GROUP * LANES))
    ocmp, sel = cmp_select_prompt(z, c_q // KVD, kc_bd, vc_bd, bias_cmp, batch=batch, seq=seq)
    tile_dist = (jnp.array([TQ, 0])[:, None, None] + jnp.arange(TQ)[None, None, :] - jnp.arange(TQ)[None, :, None])
    bias_tile = _bias_lookup(heads.T, tile_dist)
    o_p = attn_prompt(heads[N_BUCKETS - 1], z, c_q // KVD, c_kv // KVD + 2, sel, ocmp, ng, bias_tile,
                      batch=batch, seq=seq)
    hp = merge(a_p, o_p, gates, hp, wpp, wnp, wo, row(g_post_mix))
    hp, tails = ffn(hp, None, row(g_pre_ffn), w_up16, conv_w, row(conv_b), wd, row(g_post_ffn),
                    tm=512, tf=512, halo=16, shift=1, tiles_per_seq=seq // 512, tail=8)
    conv_new_p = tails.reshape(batch, seq // 512, 8, d_ff)[:, -1, 8 - (CONV_WIDTH - 1):]

    n_s = ts * bs
    zs = rms_matmul(hs_t, row(g_pre_mix), w_a, tm=n_s, tn=512, name="proj_main_sample")
    gates_s = rms_matmul(hs_t, row(g_pre_mix), w_mg, tm=n_s, tn=512, act="sigmoid", name="proj_gate_sample")
    ng_s = rms_matmul(hs_t, row(g_pre_mix), w_ng, tm=n_s, tn=LANES, act="sigmoid", name="proj_ng_sample")
    hist_t = pool_l.transpose(1, 0, 2)
    a_s = pool_sample(zs, hist_t, w_grp, row(pool_scale), ts=ts, bs=bs, past_len=past_len)
    zs3 = zs.reshape(ts, bs, -1)
    kv_rows_s = zs3[:, :, c_kv:c_kv + 4 * KVD].transpose(1, 0, 2).reshape(bs, ts, N_PAGED_SLOTS, N_KV_HEADS, HEAD_DIM)
    win_rows = zs3[:, :, c_kv + 4 * KVD:c_ng].transpose(1, 0, 2).reshape(bs, ts, 2, N_KV_HEADS, HEAD_DIM)
    win_new_s = jnp.concatenate([win_l, win_rows], axis=1)[:, -win_l.shape[1]:]
    pool_new_s = jnp.concatenate([pool_l, zs3[:, :, :pool_w].transpose(1, 0, 2)], axis=1)[:, -POOL_HIST:]

    cache_t = cache_l.transpose(0, 2, 3, 4, 1).reshape(cache_l.shape[0], N_PAGED_SLOTS * KVD, page)
    cmp_s = compress_sample(cache_t, pt_flat, perm, pe_t, w1bd, w2bd, bs=bs, n_pages=n_pages)
    cmp_s = jnp.pad(cmp_s, ((0, 0), (0, 0), (0, LANES - cmp_s.shape[2]), (0, 0))).astype(BF16)

    rows = GROUP * N_KV_HEADS * ts
    q_s = zs3[:, :, c_q:c_kv].reshape(ts, bs, N_KV_HEADS, GROUP, HEAD_DIM).transpose(1, 3, 2, 0, 4)
    q_bd = jnp.einsum("bgktd,kj->bgktjd", q_s * ATTN_SCALE, jnp.eye(N_KV_HEADS, dtype=F32))
    q_bd = q_bd.reshape(bs, rows, KVD).astype(BF16)
    pad_rows = lambda x: jnp.pad(x.transpose(1, 0, 2), ((0, 0), (0, 16 - ts), (0, 0))).astype(BF16)
    knew = pad_rows(zs3[:, :, c_kv + 2 * KVD:c_kv + 4 * KVD])
    wnew = pad_rows(zs3[:, :, c_kv + 4 * KVD:c_ng])
    wb = win_l.shape[1]
    win_t = win_l.transpose(0, 2, 3, 4, 1).reshape(bs, 2 * KVD, wb)
    r_idx = jnp.arange(rows)
    r_head = ((r_idx % (N_KV_HEADS * ts)) // ts) * GROUP + r_idx // (N_KV_HEADS * ts)
    r_pos = past_len + r_idx % ts
    row_tab = heads.T[r_head]
    rbias = lambda dist: _bias_lookup_rows(row_tab, dist)
    bcmp = rbias(_cmp_dist(r_pos))
    bsel = rbias(r_pos[:, None] - jnp.arange(past_len)[None, :])
    bnew = rbias((r_idx % ts)[:, None] - jnp.arange(16)[None, :])
    bwin = rbias(r_pos[:, None] - (past_len - wb + jnp.arange(wb))[None, :])
    expand = (jnp.arange(LANES)[:, None] == SEL_PER * (jnp.arange(past_len) // SEL_BLOCK)[None, :]).astype(BF16)
    ng_rows = ng_s[:, :3 * N_HEADS].reshape(ts, bs, 3, N_KV_HEADS, GROUP).transpose(1, 4, 3, 0, 2).reshape(bs, rows, 3)
    o_s = attn_sample(cache_t, pt_flat, q_bd, knew, win_t, wnew, cmp_s[:, 0], cmp_s[:, 1], bcmp, bsel, bnew, bwin,
                      bnew, expand, ng_rows, bs=bs, ts=ts, n_pages=n_pages, past_len=past_len)
    o_s = o_s.reshape(bs, GROUP, N_KV_HEADS, ts, HEAD_DIM).transpose(3, 0, 2, 1, 4).reshape(n_s, q_w)
    hs_t = merge(a_s, o_s, gates_s, hs_t, wpp, wnp, wo, row(g_post_mix))
    conv_hist = conv_l.transpose(1, 0, 2).reshape((CONV_WIDTH - 1) * bs, d_ff)
    hs_t, tails_s = ffn(hs_t, conv_hist, row(g_pre_ffn), w_up16, conv_w, row(conv_b), wd, row(g_post_ffn),
                        tm=n_s, tf=512, halo=(CONV_WIDTH - 1) * bs, shift=bs, tiles_per_seq=1,
                        tail=(CONV_WIDTH - 1) * bs)
    conv_new_s = tails_s.reshape(CONV_WIDTH - 1, bs, d_ff).transpose(1, 0, 2)
    return hp, hs_t, (kv_rows_p, kv_rows_s, win_p, win_new_s, pool_new_p, pool_new_s, conv_new_p, conv_new_s)


def kernel(x_prompt, x_sample, cache_kv, page_table, state_kv_win, state_pool, state_conv, g_pre_mix, w_in, pe_cmp_k, w1_cmp_k, w2_cmp_k, pe_cmp_v, w1_cmp_v, w2_cmp_v, rel_bias, w_pool_grp, pool_scale, w_pool_proj, w_nsa_proj, w_out, g_post_mix, g_pre_ffn, w_up, conv_w, conv_b, w_down, g_post_ffn):
    batch, seq, d_model = x_prompt.shape
    bs, ts, _ = x_sample.shape
    depth = cache_kv.shape[0]
    n_pages, page = page_table.shape[1], cache_kv.shape[2]
    assert ts >= CONV_WIDTH - 1
    dims = (batch, seq, bs, ts, n_pages, page)
    pt_flat = page_table.reshape(-1).astype(jnp.int32)
    hp = x_prompt.reshape(batch * seq, d_model)
    hs_t = x_sample.transpose(1, 0, 2).reshape(ts * bs, d_model)
    outs = [[] for _ in range(8)]
    per_layer = (g_pre_mix, w_in, pe_cmp_k, w1_cmp_k, w2_cmp_k, pe_cmp_v, w1_cmp_v, w2_cmp_v, w_pool_grp, pool_scale,
                 w_pool_proj, w_nsa_proj, w_out, g_post_mix, g_pre_ffn, w_up, conv_w, conv_b, w_down, g_post_ffn)
    for l in range(depth):
        wts = tuple(w[l] for w in per_layer)
        hp, hs_t, states = _layer(hp, hs_t, cache_kv[l], pt_flat, state_kv_win[l], state_pool[l], state_conv[l],
                                  wts, rel_bias, dims)
        for acc, s in zip(outs, states):
            acc.append(s)
    y_prompt = hp.reshape(batch, seq, d_model)
    y_sample = hs_t.reshape(ts, bs, d_model).transpose(1, 0, 2)
    return (y_prompt, y_sample) + tuple(jnp.stack(o) for o in outs)
```

```python
import functools
import math

import jax
import jax.numpy as jnp
from jax import lax
from jax.experimental import pallas as pl
from jax.experimental.pallas import tpu as pltpu

F32 = jnp.float32
BF16 = jnp.bfloat16

EPS = 1e-6
POOL_WINDOWS = (2, 4, 8, 16)
POOL_HIST = max(POOL_WINDOWS) - 1
N_HEADS = 16
HEAD_DIM = 64
N_KV_HEADS = 4
GROUP = N_HEADS // N_KV_HEADS
KVD = N_KV_HEADS * HEAD_DIM
N_PAGED_SLOTS = 4
CMP_BLOCK = 32
SEL_BLOCK = 64
SEL_PER = SEL_BLOCK // CMP_BLOCK
N_SELECT = 16
WINDOW = 512
FORCE_BONUS = 1.0e4
ATTN_SCALE = HEAD_DIM ** -0.5
N_BUCKETS = 32
MAX_DISTANCE = 128
CONV_WIDTH = 3
NEG = -1e30
LANES = 128
TQ = 256
SM_LANES = 128
V_ROWS = HEAD_DIM + 16
CMP_GROUP = 4
FF_PARTS = 2

_NT = (((1,), (1,)), ((), ()))


def _cparams(sem, vmem_mb):
    return pltpu.CompilerParams(dimension_semantics=sem, vmem_limit_bytes=vmem_mb * 1024 * 1024)


def _rms(x, g):
    ms = jnp.mean(x * x, axis=-1, keepdims=True)
    return x * lax.rsqrt(ms + EPS) * g


def _dot(a, b):
    return jnp.dot(a, b, preferred_element_type=F32)


def _dot_nt(a, b):
    return lax.dot_general(a, b, _NT, preferred_element_type=F32)


def _bucket(dist):
    n = jnp.maximum(dist, 0)
    max_exact = N_BUCKETS // 2
    nf = jnp.maximum(n, 1).astype(F32)
    large = max_exact + (jnp.log(nf / max_exact) / math.log(MAX_DISTANCE / max_exact)
                         * (N_BUCKETS - max_exact)).astype(jnp.int32)
    large = jnp.minimum(large, N_BUCKETS - 1)
    return jnp.where(n < max_exact, n, large)


def _bias_lookup(tab, dist):
    bkt = _bucket(dist)[None]
    col = lambda k: tab[:, k].reshape((-1,) + (1,) * dist.ndim)
    out = jnp.broadcast_to(col(N_BUCKETS - 1), (tab.shape[0],) + dist.shape)
    for k in range(N_BUCKETS - 1):
        out = jnp.where(bkt == k, col(k), out)
    return out


def _bias_lookup_rows(tab, dist):
    bkt = _bucket(dist)
    out = jnp.broadcast_to(tab[:, N_BUCKETS - 1:], dist.shape)
    for k in range(N_BUCKETS - 1):
        out = jnp.where(bkt == k, tab[:, k:k + 1], out)
    return out


def _softmax_parts(s, mask):
    s = jnp.where(mask, s, NEG)
    m = jnp.max(s, axis=-1, keepdims=True)
    e = jnp.where(mask, jnp.exp(s - m), 0.0)
    d = jnp.sum(e, axis=-1, keepdims=True)
    return e, d


def _topk_mask(score, n_blocks, k):
    lane = lax.broadcasted_iota(jnp.int32, score.shape, 1)
    rank = jnp.zeros(score.shape, F32)
    for i in range(n_blocks):
        ci = score[:, SEL_PER * i:SEL_PER * i + 1]
        rank = rank + jnp.where(lane > SEL_PER * i, jnp.where(ci >= score, 1.0, 0.0), jnp.where(ci > score, 1.0, 0.0))
    return jnp.where((rank < k) & (lane % SEL_PER == 0) & (lane < SEL_PER * n_blocks), 1.0, 0.0)


def _block_valid(shape, q_pos, n_sb):
    lane = lax.broadcasted_iota(jnp.int32, shape, 1)
    blk = lane // SEL_PER
    return (lane % SEL_PER == 0) & (blk * SEL_BLOCK <= q_pos) & (blk < n_sb)


def _block_scores(p_sum, q_pos, n_sb):
    imp = p_sum
    for k in range(1, SEL_PER):
        imp = imp + pltpu.roll(p_sum, LANES - k, axis=1)
    blk = lax.broadcasted_iota(jnp.int32, imp.shape, 1) // SEL_PER
    cur = q_pos // SEL_BLOCK
    forced = (blk == 0) | (blk == cur) | (blk == cur - 1)
    return jnp.where(_block_valid(imp.shape, q_pos, n_sb), imp + FORCE_BONUS * forced.astype(F32), -jnp.inf)


def _rms_matmul_kernel(x_ref, g_ref, w_ref, o_ref, xn_ref, *, act):
    @pl.when(pl.program_id(1) == 0)
    def _():
        xn_ref[...] = _rms(x_ref[...], g_ref[...]).astype(BF16)

    acc = _dot(xn_ref[...], w_ref[...])
    if act == "sigmoid":
        acc = jax.nn.sigmoid(acc)
    o_ref[...] = acc.astype(o_ref.dtype)


def rms_matmul(x, g, w, *, tm, tn, act=None, name):
    n, k = x.shape
    m = w.shape[1]
    return pl.pallas_call(
        functools.partial(_rms_matmul_kernel, act=act),
        grid=(n // tm, m // tn),
        in_specs=[pl.BlockSpec((tm, k), lambda i, j: (i, 0)),
                  pl.BlockSpec((1, k), lambda i, j: (0, 0)),
                  pl.BlockSpec((k, tn), lambda i, j: (0, j))],
        out_specs=pl.BlockSpec((tm, tn), lambda i, j: (i, j)),
        out_shape=jax.ShapeDtypeStruct((n, m), F32),
        scratch_shapes=[pltpu.VMEM((tm, k), BF16)],
        compiler_params=_cparams(("arbitrary", "arbitrary"), 48),
        name=name,
    )(x, g, w)


def _pool_prompt_kernel(x_ref, h_ref, w_ref, sc_ref, o_ref, buf_ref, *, tt):
    i = pl.program_id(1)
    pg = w_ref.shape[1]
    buf_ref[0:16, :] = jnp.where(i == 0, 0.0, h_ref[...])
    buf_ref[16:, :] = x_ref[...]
    pos = i * tt + lax.broadcasted_iota(jnp.int32, (tt, 1), 0)
    for gi, w in enumerate(POOL_WINDOWS):
        c = slice(gi * pg, (gi + 1) * pg)
        x = buf_ref[16:, c]
        win = x
        for k in range(1, w):
            win = win + buf_ref[16 - k:16 - k + tt, c]
        cnt = jnp.minimum(w, pos + 1).astype(F32)
        pooled = win / cnt - x
        o_ref[:, c] = (_dot(pooled.astype(BF16), w_ref[gi]) * sc_ref[:, c]).astype(o_ref.dtype)


def pool_prompt(z, w_grp, scale, *, batch, seq, tt):
    pw = w_grp.shape[0] * w_grp.shape[1]
    nt = seq // tt
    return pl.pallas_call(
        functools.partial(_pool_prompt_kernel, tt=tt),
        grid=(batch, nt),
        in_specs=[pl.BlockSpec((tt, pw), lambda b, i: (b * nt + i, 0)),
                  pl.BlockSpec((16, pw), lambda b, i: (jnp.maximum((b * nt + i) * (tt // 16) - 1, 0), 0)),
                  pl.BlockSpec(w_grp.shape, lambda b, i: (0, 0, 0)),
                  pl.BlockSpec((1, pw), lambda b, i: (0, 0))],
        out_specs=pl.BlockSpec((tt, pw), lambda b, i: (b * nt + i, 0)),
        out_shape=jax.ShapeDtypeStruct((batch * seq, pw), BF16),
        scratch_shapes=[pltpu.VMEM((tt + 16, pw), F32)],
        compiler_params=_cparams(("arbitrary", "arbitrary"), 32),
        name="pool_prompt",
    )(z, z, w_grp, scale)


def _pool_sample_kernel(x_ref, h_ref, w_ref, sc_ref, o_ref, *, ts, bs, past_len):
    pg = w_ref.shape[1]

    def row(j, c):
        if j < POOL_HIST:
            return h_ref[j, :, c]
        return x_ref[(j - POOL_HIST) * bs:(j - POOL_HIST + 1) * bs, c]

    for t in range(ts):
        for gi, w in enumerate(POOL_WINDOWS):
            c = slice(gi * pg, (gi + 1) * pg)
            x = row(POOL_HIST + t, c)
            win = x
            for k in range(1, w):
                win = win + row(POOL_HIST + t - k, c)
            cnt = float(min(w, past_len + t + 1))
            pooled = win / cnt - x
            o_ref[t * bs:(t + 1) * bs, c] = (_dot(pooled.astype(BF16), w_ref[gi]) * sc_ref[:, c]).astype(o_ref.dtype)


def pool_sample(z, hist_t, w_grp, scale, *, ts, bs, past_len):
    pw = w_grp.shape[0] * w_grp.shape[1]
    return pl.pallas_call(
        functools.partial(_pool_sample_kernel, ts=ts, bs=bs, past_len=past_len),
        grid=(1,),
        in_specs=[pl.BlockSpec((ts * bs, pw), lambda i: (0, 0)),
                  pl.BlockSpec(hist_t.shape, lambda i: (0, 0, 0)),
                  pl.BlockSpec(w_grp.shape, lambda i: (0, 0, 0)),
                  pl.BlockSpec((1, pw), lambda i: (0, 0))],
        out_specs=pl.BlockSpec((ts * bs, pw), lambda i: (0, 0)),
        out_shape=jax.ShapeDtypeStruct((ts * bs, pw), BF16),
        compiler_params=_cparams(("arbitrary",), 48),
        name="pool_sample",
    )(z, hist_t, w_grp, scale)


def _compress_tail(hid, w2_ref):
    hid = jax.nn.gelu(hid, approximate=True)
    return _dot(hid.astype(BF16), w2_ref[...])


def _compress_prompt_kernel(x_ref, perm_ref, pe_ref, w1_ref, w2_ref, o_ref, xs_ref, *, n_blk, chunk):
    n_chunks = x_ref.shape[0] // chunk
    blk_chunk = chunk // CMP_BLOCK
    for s in range(2):
        c = slice(s * KVD, (s + 1) * KVD)
        for ch in range(n_chunks):
            x = (x_ref[ch * chunk:(ch + 1) * chunk, c] + pe_ref[s]).astype(BF16)
            xs_ref[ch] = _dot(perm_ref[...], x).reshape(CMP_BLOCK, blk_chunk, KVD)
        hid = jnp.zeros((n_blk, KVD), F32)
        for l in range(CMP_BLOCK):
            hid = hid + _dot(xs_ref[:, l].reshape(n_blk, KVD).astype(BF16), w1_ref[s, l])
        o_ref[0, s] = _compress_tail(hid, w2_ref.at[s])


def compress_prompt(z, col_block, perm, pe_n, w1bd, w2bd, *, batch, seq):
    n_blk = seq // CMP_BLOCK
    chunk = perm.shape[0]
    assert seq % chunk == 0
    return pl.pallas_call(
        functools.partial(_compress_prompt_kernel, n_blk=n_blk, chunk=chunk),
        grid=(batch,),
        in_specs=[pl.BlockSpec((seq, 2 * KVD), lambda b: (b, col_block)),
                  pl.BlockSpec(perm.shape, lambda b: (0, 0)),
                  pl.BlockSpec(pe_n.shape, lambda b: (0, 0, 0)),
                  pl.BlockSpec(w1bd.shape, lambda b: (0, 0, 0, 0), pipeline_mode=pl.Buffered(1)),
                  pl.BlockSpec(w2bd.shape, lambda b: (0, 0, 0))],
        out_specs=pl.BlockSpec((1, 2, n_blk, KVD), lambda b: (b, 0, 0, 0)),
        out_shape=jax.ShapeDtypeStruct((batch, 2, n_blk, KVD), F32),
        scratch_shapes=[pltpu.VMEM((seq // chunk, CMP_BLOCK, chunk // CMP_BLOCK, KVD), F32)],
        compiler_params=_cparams(("arbitrary",), 48),
        name="compress_prompt",
    )(z, perm, pe_n, w1bd, w2bd)


def _compress_sample_kernel(pt_ref, *refs, n_pages, page):
    del pt_ref
    page_refs = refs[:n_pages]
    perm_ref, pe_ref, w1_ref, w2_ref, o_ref, x_ref = refs[n_pages:]
    b = pl.program_id(0)
    bl = b % CMP_GROUP
    n_pairs = n_pages // 2
    blk_pair = 2 * page // CMP_BLOCK
    for pp in range(n_pairs):
        for s in range(2):
            r = slice(s * KVD, (s + 1) * KVD)
            xt = jnp.concatenate([page_refs[2 * pp][0, r, :], page_refs[2 * pp + 1][0, r, :]], axis=1)
            xt = (xt + pe_ref[s]).astype(BF16)
            xp = _dot_nt(perm_ref[...], xt)
            x_ref[s, bl * n_pairs + pp] = xp.reshape(CMP_BLOCK, blk_pair, KVD)

    @pl.when(bl == CMP_GROUP - 1)
    def _():
        rows = CMP_GROUP * n_pairs * blk_pair
        for s in range(2):
            hid = jnp.zeros((rows, KVD), F32)
            for l in range(CMP_BLOCK):
                xl = x_ref[s, :, l].reshape(rows, KVD)
                hid = hid + _dot(xl.astype(BF16), w1_ref[s, l])
            o_ref[:, s] = _compress_tail(hid, w2_ref.at[s]).reshape(CMP_GROUP, n_pairs * blk_pair, KVD)


def compress_sample(cache_t, pt_flat, perm, pe_t, w1bd, w2bd, *, bs, n_pages):
    page = cache_t.shape[2]
    assert page == LANES and n_pages % 2 == 0 and bs % CMP_GROUP == 0
    n_pairs = n_pages // 2
    blk_pair = 2 * page // CMP_BLOCK
    n_blk = n_pairs * blk_pair

    def page_map(b, pt, *, p):
        return (pt[b * n_pages + p], 0, 0)

    const = lambda nd: (lambda b, pt: (0,) * nd)
    grid_spec = pltpu.PrefetchScalarGridSpec(
        num_scalar_prefetch=1,
        grid=(bs,),
        in_specs=[pl.BlockSpec((1, 2 * KVD, page), functools.partial(page_map, p=p)) for p in range(n_pages)]
        + [pl.BlockSpec(perm.shape, const(2)), pl.BlockSpec(pe_t.shape, const(3)),
           pl.BlockSpec(w1bd.shape, const(4), pipeline_mode=pl.Buffered(1)), pl.BlockSpec(w2bd.shape, const(3))],
        out_specs=pl.BlockSpec((CMP_GROUP, 2, n_blk, KVD), lambda b, pt: (b // CMP_GROUP, 0, 0, 0)),
        scratch_shapes=[pltpu.VMEM((2, CMP_GROUP * n_pairs, CMP_BLOCK, blk_pair, KVD), F32)],
    )
    return pl.pallas_call(
        functools.partial(_compress_sample_kernel, n_pages=n_pages, page=page),
        grid_spec=grid_spec,
        out_shape=jax.ShapeDtypeStruct((bs, 2, n_blk, KVD), F32),
        compiler_params=_cparams(("arbitrary",), 56),
        name="compress_sample",
    )(pt_flat, *([cache_t] * n_pages), perm, pe_t, w1bd, w2bd)


def _cmp_valid(q_pos, shape, n_cmp):
    n = lax.broadcasted_iota(jnp.int32, shape, 1) % LANES
    return (n < n_cmp) & (n * CMP_BLOCK + CMP_BLOCK - 1 <= q_pos)


def _cmp_select_prompt_kernel(q_ref, kc_ref, vc_ref, bias_ref, ocmp_ref, sel_ref, *, n_cmp, n_sb):
    qb = pl.program_id(2)
    q_pos = qb * TQ + lax.broadcasted_iota(jnp.int32, (TQ, 1), 0)
    valid = _cmp_valid(q_pos, (TQ, LANES), n_cmp)
    q = (q_ref[...] * ATTN_SCALE).astype(BF16)
    s = _dot(q, kc_ref[0, 0]) + bias_ref[0]
    p_sum = jnp.zeros((TQ, LANES), F32)
    probs = []
    for g in range(GROUP):
        e, d = _softmax_parts(s[:, g * LANES:(g + 1) * LANES], valid)
        p = e / jnp.where(d > 0, d, 1.0)
        probs.append(p.astype(BF16))
        p_sum = p_sum + p
    ocmp_ref[...] = _dot(jnp.concatenate(probs, axis=1), vc_ref[0, 0])

    k_sel = min(N_SELECT, n_sb)
    srows = SEL_PER * n_sb
    row = lax.broadcasted_iota(jnp.int32, (srows, TQ), 0)
    pos_t = qb * TQ + lax.broadcasted_iota(jnp.int32, (1, TQ), 1)
    blk = row // SEL_PER
    started = (row % SEL_PER == 0) & (blk * SEL_BLOCK <= pos_t)
    all_fit = (qb + 1) * TQ <= k_sel * SEL_BLOCK
    sel_ref[0, 0, 0] = jnp.zeros((LANES, TQ), F32)

    @pl.when(all_fit)
    def _():
        sel_ref[0, 0, 0, 0:srows, :] = jnp.where(started, 1.0, 0.0)

    @pl.when(jnp.logical_not(all_fit))
    def _():
        pt = p_sum.T
        imp = pt[0:srows]
        for k in range(1, SEL_PER):
            imp = imp + pltpu.roll(pt, LANES - k, axis=0)[0:srows]
        cur = pos_t // SEL_BLOCK
        forced = (blk == 0) | (blk == cur) | (blk == cur - 1)
        score = jnp.where(started, imp + FORCE_BONUS * forced.astype(F32), -jnp.inf)
        rank = jnp.zeros((srows, TQ), F32)
        for i in range(n_sb):
            ci = score[SEL_PER * i:SEL_PER * i + 1, :]
            rank = rank + jnp.where(row > SEL_PER * i, jnp.where(ci >= score, 1.0, 0.0), jnp.where(ci > score, 1.0, 0.0))
        sel_ref[0, 0, 0, 0:srows, :] = jnp.where((rank < k_sel) & (row % SEL_PER == 0), 1.0, 0.0)


def cmp_select_prompt(z, q_col, kc_bd, vc_bd, bias_cmp, *, batch, seq):
    n_cmp = seq // CMP_BLOCK
    n_sb = -(-seq // SEL_BLOCK)
    nq = seq // TQ
    gw = GROUP * HEAD_DIM
    assert SEL_PER * n_sb <= LANES and n_cmp <= LANES
    return pl.pallas_call(
        functools.partial(_cmp_select_prompt_kernel, n_cmp=n_cmp, n_sb=n_sb),
        grid=(batch, N_KV_HEADS, nq),
        in_specs=[pl.BlockSpec((TQ, gw), lambda b, kv, i: (b * nq + i, q_col + kv)),
                  pl.BlockSpec((1, 1, gw, GROUP * LANES), lambda b, kv, i: (b, kv, 0, 0)),
                  pl.BlockSpec((1, 1, GROUP * LANES, gw), lambda b, kv, i: (b, kv, 0, 0)),
                  pl.BlockSpec((1, TQ, GROUP * LANES), lambda b, kv, i: (kv, i, 0))],
        out_specs=[pl.BlockSpec((TQ, gw), lambda b, kv, i: (b * nq + i, kv)),
                   pl.BlockSpec((1, 1, 1, LANES, TQ), lambda b, kv, i: (b, kv, i, 0, 0))],
        out_shape=[jax.ShapeDtypeStruct((batch * seq, N_HEADS * HEAD_DIM), F32),
                   jax.ShapeDtypeStruct((batch, N_KV_HEADS, nq, LANES, TQ), F32)],
        compiler_params=_cparams(("arbitrary",) * 3, 32),
        name="cmp_select_prompt",
    )(z, kc_bd, vc_bd, bias_cmp)


def _attn_prompt_kernel(far_ref, q_ref, ks_ref, vs_ref, kw_ref, vw_ref, sel_ref, ocmp_ref, ng_ref, bt_ref, o_ref,
                        qbd_ref, ngt_ref, *state_refs):
    kv = pl.program_id(1)
    qb = pl.program_id(2)
    gw = GROUP * HEAD_DIM
    sel_state, win_state = state_refs[:5], state_refs[5:]
    blk_of_key = lax.broadcasted_iota(jnp.int32, (TQ, LANES), 0) // SEL_BLOCK
    blk_lane = lax.broadcasted_iota(jnp.int32, (TQ, LANES), 1)
    near0 = jnp.maximum(qb - 1, 0)
    eye = (lax.broadcasted_iota(jnp.int32, (gw, gw), 0) == lax.broadcasted_iota(jnp.int32, (gw, gw), 1)).astype(BF16)
    qt = _dot_nt(eye, (q_ref[...] * ATTN_SCALE).astype(BF16)).astype(BF16)
    qbd_ref[0:KVD, :] = jnp.zeros((KVD, GROUP * TQ), BF16)
    row0 = pl.multiple_of(kv * HEAD_DIM, HEAD_DIM)
    sel_neg = ((sel_ref[0, 0, 0] - 1.0) * (-NEG)).astype(BF16)
    for g in range(GROUP):
        qbd_ref[pl.ds(row0, HEAD_DIM), g * TQ:(g + 1) * TQ] = qt[g * HEAD_DIM:(g + 1) * HEAD_DIM, :]
        qbd_ref[KVD:, g * TQ:(g + 1) * TQ] = sel_neg
    vrow = lax.broadcasted_iota(jnp.int32, (V_ROWS, KVD), 0)
    vpick = ((lax.broadcasted_iota(jnp.int32, (V_ROWS, KVD), 1) == kv * HEAD_DIM + vrow) & (vrow < HEAD_DIM))
    vpick = vpick.astype(BF16)
    ones_row = lax.broadcasted_iota(jnp.int32, (V_ROWS, TQ), 0) == HEAD_DIM

    def chunk(state, j, k_ref, v_ref, selected, tile):
        s_ref, p_ref, m_ref, a_ref, acc_ref = state
        start = pl.multiple_of(j * TQ, TQ)
        k = k_ref[pl.ds(start, TQ), :].astype(BF16)
        if selected:
            onehot = (blk_lane == SEL_PER * (blk_of_key + j * (TQ // SEL_BLOCK))).astype(BF16)
            s_ref[...] = _dot(jnp.concatenate([k, onehot], axis=1), qbd_ref[...])
        else:
            s_ref[...] = _dot(k, qbd_ref[0:KVD, :])
        vt = jnp.where(ones_row, 1.0, _dot_nt(vpick, v_ref[pl.ds(start, TQ), :].astype(BF16))).astype(BF16)
        for g in range(GROUP):
            far = far_ref[kv * GROUP + g]
            for half in range(TQ // SM_LANES):
                c = slice(g * TQ + half * SM_LANES, g * TQ + (half + 1) * SM_LANES)
                u = s_ref[:, c]
                if tile is not None:
                    u = u + bt_ref[g, tile, :, half * SM_LANES:(half + 1) * SM_LANES]
                m_old = m_ref[:, c]
                mx = jnp.max(u, axis=0, keepdims=True)
                m_new = jnp.maximum(m_old, mx + far if tile is None else mx)
                a_ref[:, c] = jnp.exp(m_old - m_new)
                m_ref[:, c] = m_new
                p_ref[:, c] = jnp.exp(u - (m_new - far if tile is None else m_new)).astype(BF16)
        acc_ref[...] = a_ref[...] * acc_ref[...] + _dot(vt, p_ref[...])

    def finish(state):
        acc_ref = state[4]
        l = acc_ref[HEAD_DIM:HEAD_DIM + 1, :]
        return acc_ref[0:HEAD_DIM, :] / jnp.where(l > 0, l, 1.0)

    for state in (sel_state, win_state):
        state[2][...] = jnp.full(state[2].shape, NEG, F32)
        state[4][...] = jnp.zeros(state[4].shape, F32)

    def far_body(j, c):
        chunk(sel_state, j, ks_ref, vs_ref, True, None)
        return c
    lax.fori_loop(0, near0, far_body, 0)

    @pl.when(qb >= WINDOW // TQ)
    def _():
        chunk(win_state, qb - WINDOW // TQ, kw_ref, vw_ref, False, 0)

    def near_body(j, c):
        tile = j - (qb - 1) + 1
        chunk(sel_state, j, ks_ref, vs_ref, True, tile)
        chunk(win_state, j, kw_ref, vw_ref, False, tile)
        return c
    lax.fori_loop(near0, qb + 1, near_body, 0)
    o_sel = finish(sel_state)
    o_win = finish(win_state)
    ngt_ref[...] = ng_ref[...].T
    ocmp_t = ocmp_ref[...].T
    outs = []
    for g in range(GROUP):
        gate = lambda r: ngt_ref[pl.ds(r * N_HEADS + kv * GROUP + g, 1), :]
        c = slice(g * TQ, (g + 1) * TQ)
        outs.append(gate(0) * ocmp_t[g * HEAD_DIM:(g + 1) * HEAD_DIM, :] + gate(1) * o_sel[:, c]
                    + gate(2) * o_win[:, c])
    o_t = jnp.concatenate(outs, axis=0).astype(BF16)
    o_ref[...] = _dot_nt(eye, o_t).astype(o_ref.dtype)


def attn_prompt(far, z, q_col, kv_col, sel, ocmp, ng, bias_tile, *, batch, seq):
    assert WINDOW == 2 * TQ and seq % TQ == 0 and TQ % SM_LANES == 0 and TQ == GROUP * HEAD_DIM == KVD
    cols = GROUP * TQ
    nq = seq // TQ
    kv_spec = lambda slot: pl.BlockSpec((seq, KVD), lambda b, kv, i: (b, kv_col + slot))
    tok_spec = lambda col0: pl.BlockSpec((TQ, KVD), lambda b, kv, i: (b * nq + i, col0 + kv))
    return pl.pallas_call(
        _attn_prompt_kernel,
        grid=(batch, N_KV_HEADS, nq),
        in_specs=[pl.BlockSpec(memory_space=pltpu.SMEM),
                  tok_spec(q_col), kv_spec(0), kv_spec(1), kv_spec(2), kv_spec(3),
                  pl.BlockSpec((1, 1, 1, LANES, TQ), lambda b, kv, i: (b, kv, i, 0, 0)),
                  tok_spec(0),
                  pl.BlockSpec((TQ, LANES), lambda b, kv, i: (b * nq + i, 0)),
                  pl.BlockSpec((GROUP, 3, TQ, TQ), lambda b, kv, i: (kv, 0, 0, 0))],
        out_specs=tok_spec(0),
        out_shape=jax.ShapeDtypeStruct((batch * seq, N_HEADS * HEAD_DIM), BF16),
        scratch_shapes=[pltpu.VMEM((KVD + LANES, cols), BF16), pltpu.VMEM((LANES, TQ), F32)]
        + 2 * [pltpu.VMEM((TQ, cols), F32), pltpu.VMEM((TQ, cols), BF16),
               pltpu.VMEM((1, cols), F32), pltpu.VMEM((1, cols), F32), pltpu.VMEM((V_ROWS, cols), F32)],
        compiler_params=_cparams(("arbitrary",) * 3, 48),
        name="attn_prompt",
    )(far, z, z, z, z, z, sel, ocmp, ng, bias_tile)


def _diag_heads(x, ts):
    rows = x.shape[0]
    kv_of_row = (lax.broadcasted_iota(jnp.int32, (rows, HEAD_DIM), 0) % (N_KV_HEADS * ts)) // ts
    out = jnp.zeros((rows, HEAD_DIM), F32)
    for kv in range(N_KV_HEADS):
        out = out + jnp.where(kv_of_row == kv, x[:, kv * HEAD_DIM:(kv + 1) * HEAD_DIM], 0.0)
    return out


def _attn_sample_kernel(pt_ref, *refs, n_pages, page, ts, past_len, wb):
    del pt_ref
    page_refs = refs[:n_pages]
    (q_ref, knew_ref, win_ref, wnew_ref, kc_ref, vc_ref, bcmp_ref, bsel_ref, bseln_ref, bwin_ref, bwinn_ref,
     expand_ref, ng_ref, o_ref, kt_ref, vt_ref) = refs[n_pages:]
    rows = GROUP * N_KV_HEADS * ts
    kvt = N_KV_HEADS * ts
    n_cmp = past_len // CMP_BLOCK
    n_sb = -(-(past_len + ts) // SEL_BLOCK)
    q = q_ref[0]
    t_of_row = lax.broadcasted_iota(jnp.int32, (rows, 1), 0) % ts
    q_pos = past_len + t_of_row

    s = _dot_nt(q, kc_ref[0]) + bcmp_ref[...]
    e, d = _softmax_parts(s, _cmp_valid(q_pos, (rows, LANES), n_cmp))
    p = e / jnp.where(d > 0, d, 1.0)
    o_cmp = _diag_heads(_dot(p.astype(BF16), vc_ref[0]), ts)
    p_sum = p[0:kvt]
    for g in range(1, GROUP):
        p_sum = p_sum + p[g * kvt:(g + 1) * kvt]
    sel = _topk_mask(_block_scores(p_sum, q_pos[0:kvt], n_sb), n_sb, min(N_SELECT, n_sb))
    sel = jnp.concatenate([sel] * GROUP, axis=0)

    for pg in range(n_pages):
        kt_ref[:, pg * page:(pg + 1) * page] = page_refs[pg][0, 0:KVD, :].astype(BF16)
        vt_ref[:, pg * page:(pg + 1) * page] = page_refs[pg][0, KVD:2 * KVD, :].astype(BF16)
    new_col = lax.broadcasted_iota(jnp.int32, (rows, 16), 1)
    new_ok = (new_col <= t_of_row) & (new_col < ts)
    s = _dot(q, kt_ref[...]) + bsel_ref[...]
    mask = _dot(sel.astype(BF16), expand_ref[...]) > 0.5
    sn = _dot_nt(q, knew_ref[0, :, 0:KVD]) + bseln_ref[...]
    new_lane = SEL_PER * (past_len // SEL_BLOCK)
    mask_n = new_ok & (sel[:, new_lane:new_lane + 1] > 0.5)
    o_sel = _two_part_attention(s, mask, vt_ref[...], sn, mask_n, knew_ref[0, :, KVD:2 * KVD], ts)

    s = _dot(q, win_ref[0, 0:KVD, :].astype(BF16)) + bwin_ref[...]
    dist = wb + t_of_row - lax.broadcasted_iota(jnp.int32, (rows, wb), 1)
    mask = (dist >= 0) & (dist <= WINDOW)
    sn = _dot_nt(q, wnew_ref[0, :, 0:KVD]) + bwinn_ref[...]
    o_win = _two_part_attention(s, mask, win_ref[0, KVD:2 * KVD, :].astype(BF16), sn, new_ok,
                                wnew_ref[0, :, KVD:2 * KVD], ts)

    ng = ng_ref[0]
    o_ref[0] = (ng[:, 0:1] * o_cmp + ng[:, 1:2] * o_sel + ng[:, 2:3] * o_win).astype(o_ref.dtype)


def _two_part_attention(s, mask, vt, sn, mask_n, vn, ts):
    s = jnp.where(mask, s, NEG)
    sn = jnp.where(mask_n, sn, NEG)
    m = jnp.maximum(jnp.max(s, axis=-1, keepdims=True), jnp.max(sn, axis=-1, keepdims=True))
    e = jnp.where(mask, jnp.exp(s - m), 0.0)
    en = jnp.where(mask_n, jnp.exp(sn - m), 0.0)
    d = jnp.sum(e, axis=-1, keepdims=True) + jnp.sum(en, axis=-1, keepdims=True)
    inv = jnp.where(d > 0, d, 1.0)
    o = _dot_nt((e / inv).astype(BF16), vt) + _dot((en / inv).astype(BF16), vn)
    return _diag_heads(o, ts)


def attn_sample(cache_t, pt_flat, q_bd, knew, win_t, wnew, kc_s, vc_s, bcmp, bsel, bseln, bwin, bwinn, expand, ng_s,
                *, bs, ts, n_pages, past_len):
    page = cache_t.shape[2]
    wb = win_t.shape[2]
    rows = GROUP * N_KV_HEADS * ts
    assert past_len == n_pages * page and past_len % SEL_BLOCK == 0 and ts <= 16
    assert (past_len + ts) // CMP_BLOCK == past_len // CMP_BLOCK

    def page_map(b, pt, *, p):
        return (pt[b * n_pages + p], 1, 0)

    per_b = lambda shape: pl.BlockSpec((1,) + shape, lambda b, pt: (b,) + (0,) * len(shape))
    const = lambda a: pl.BlockSpec(a.shape, lambda b, pt: (0,) * a.ndim)
    grid_spec = pltpu.PrefetchScalarGridSpec(
        num_scalar_prefetch=1,
        grid=(bs,),
        in_specs=[pl.BlockSpec((1, 2 * KVD, page), functools.partial(page_map, p=p)) for p in range(n_pages)]
        + [per_b((rows, KVD)), per_b((16, 2 * KVD)), per_b((2 * KVD, wb)), per_b((16, 2 * KVD)),
           per_b((LANES, KVD)), per_b((LANES, KVD)),
           const(bcmp), const(bsel), const(bseln), const(bwin), const(bwinn), const(expand),
           per_b((rows, 3))],
        out_specs=per_b((rows, HEAD_DIM)),
        scratch_shapes=[pltpu.VMEM((KVD, past_len), BF16), pltpu.VMEM((KVD, past_len), BF16)],
    )
    return pl.pallas_call(
        functools.partial(_attn_sample_kernel, n_pages=n_pages, page=page, ts=ts, past_len=past_len, wb=wb),
        grid_spec=grid_spec,
        out_shape=jax.ShapeDtypeStruct((bs, rows, HEAD_DIM), BF16),
        compiler_params=_cparams(("arbitrary",), 48),
        name="attn_sample",
    )(pt_flat, *([cache_t] * n_pages), q_bd, knew, win_t, wnew, kc_s, vc_s, bcmp, bsel, bseln, bwin, bwinn, expand,
      ng_s)


def _merge_kernel(a_ref, o_ref, g0_ref, g1_ref, h_ref, wp_ref, wn_ref, wo_ref, gp_ref, out_ref):
    merged = g0_ref[...] * _dot(a_ref[...], wp_ref[...]) + g1_ref[...] * _dot(o_ref[...], wn_ref[...])
    r = _dot(merged.astype(BF16), wo_ref[...])
    out_ref[...] = h_ref[...] + _rms(r, gp_ref[...])


def merge(a, o, gates, h, wp, wn, wo, gp):
    n, d = h.shape
    tm = min(256, n)
    pw = a.shape[1]
    qw = o.shape[1]
    once = pl.Buffered(1)
    return pl.pallas_call(
        _merge_kernel,
        grid=(n // tm,),
        in_specs=[pl.BlockSpec((tm, pw), lambda i: (i, 0)),
                  pl.BlockSpec((tm, qw), lambda i: (i, 0)),
                  pl.BlockSpec((tm, d), lambda i: (i, 0)),
                  pl.BlockSpec((tm, d), lambda i: (i, 1)),
                  pl.BlockSpec((tm, d), lambda i: (i, 0)),
                  pl.BlockSpec(wp.shape, lambda i: (0, 0), pipeline_mode=once),
                  pl.BlockSpec(wn.shape, lambda i: (0, 0), pipeline_mode=once),
                  pl.BlockSpec(wo.shape, lambda i: (0, 0), pipeline_mode=once),
                  pl.BlockSpec((1, d), lambda i: (0, 0))],
        out_specs=pl.BlockSpec((tm, d), lambda i: (i, 0)),
        out_shape=jax.ShapeDtypeStruct((n, d), F32),
        compiler_params=_cparams(("arbitrary",), 56),
        name="merge",
    )(a, o, gates, gates, h, wp, wn, wo, gp)


def _ffn_kernel(h_ref, halo_ref, gpre_ref, wv_ref, wg_ref, cw_ref, cb_ref, wd_ref, gpost_ref,
                out_ref, tail_ref, xn_ref, acc_ref, *, tm, halo, shift, tiles_per_seq, tail, from_rows):
    i = pl.program_id(0)
    f = pl.program_id(1)

    @pl.when(f == 0)
    def _():
        xn_ref[halo:, :] = _rms(h_ref[...], gpre_ref[...]).astype(BF16)
        if from_rows:
            xn_ref[0:halo, :] = _rms(halo_ref[...], gpre_ref[...]).astype(BF16)
        acc_ref[...] = jnp.zeros_like(acc_ref)

    tf = wv_ref.shape[1]
    parts = [slice(k * tf // FF_PARTS, (k + 1) * tf // FF_PARTS) for k in range(FF_PARTS)]
    ups = []
    for c in parts:
        val = _dot(xn_ref[halo:, :], wv_ref[:, c])
        if from_rows:
            gcat = _dot(xn_ref[...], wg_ref[:, c])
            is_hist = lax.broadcasted_iota(jnp.int32, gcat.shape, 0) < halo
            gcat = jnp.where(is_hist & (i % tiles_per_seq == 0), 0.0, gcat)
        else:
            gcat = jnp.concatenate([halo_ref[:, c], _dot(xn_ref[halo:, :], wg_ref[:, c])], axis=0)
        ups.append((val, gcat))
    down = None
    for c, (val, gcat) in zip(parts, ups):
        conv = cb_ref[:, c]
        for j in range(CONV_WIDTH):
            back = (CONV_WIDTH - 1 - j) * shift
            if back % 8 == 0:
                tap = gcat[halo - back:halo - back + tm]
            else:
                tap = pltpu.roll(gcat, back, axis=0)[halo:]
            conv = conv + cw_ref[j:j + 1, c] * tap
        act = jax.nn.gelu(conv, approximate=True) * val
        part = _dot(act.astype(BF16), wd_ref[c, :])
        down = part if down is None else down + part
        tail_ref[:, c] = gcat[halo + tm - tail:, :]
    acc_ref[...] += down

    @pl.when(f == pl.num_programs(1) - 1)
    def _():
        out_ref[...] = h_ref[...] + _rms(acc_ref[...], gpost_ref[...])


def ffn(h, halo_src, gpre, w_up, cw, cb, wd, gpost, *, tm, tf, halo, shift, tiles_per_seq, tail):
    n, d = h.shape
    dff = wd.shape[0]
    nt = n // tm
    assert dff % tf == 0
    from_rows = halo_src is None
    if from_rows:
        halo_src = h
        halo_spec = pl.BlockSpec((halo, d), lambda i, f: (jnp.maximum(i * (tm // halo) - 1, 0), 0))
    else:
        assert nt == 1
        halo_spec = pl.BlockSpec((halo, tf), lambda i, f: (0, f))
    return pl.pallas_call(
        functools.partial(_ffn_kernel, tm=tm, halo=halo, shift=shift, tiles_per_seq=tiles_per_seq, tail=tail,
                          from_rows=from_rows),
        grid=(nt, dff // tf),
        in_specs=[pl.BlockSpec((tm, d), lambda i, f: (i, 0)),
                  halo_spec,
                  pl.BlockSpec((1, d), lambda i, f: (0, 0)),
                  pl.BlockSpec((d, tf), lambda i, f: (0, f)),
                  pl.BlockSpec((d, tf), lambda i, f: (0, dff // tf + f)),
                  pl.BlockSpec((CONV_WIDTH, tf), lambda i, f: (0, f)),
                  pl.BlockSpec((1, tf), lambda i, f: (0, f)),
                  pl.BlockSpec((tf, d), lambda i, f: (f, 0)),
                  pl.BlockSpec((1, d), lambda i, f: (0, 0))],
        out_specs=[pl.BlockSpec((tm, d), lambda i, f: (i, 0)),
                   pl.BlockSpec((tail, tf), lambda i, f: (i, f))],
        out_shape=[jax.ShapeDtypeStruct((n, d), F32),
                   jax.ShapeDtypeStruct((nt * tail, dff), F32)],
        scratch_shapes=[pltpu.VMEM((tm + halo, d), BF16), pltpu.VMEM((tm, d), F32)],
        compiler_params=_cparams(("arbitrary", "arbitrary"), 56),
        name="ffn",
    )(h, halo_src, gpre, w_up, w_up, cw, cb, wd, gpost)


def _block_diag4(w):
    eye = jnp.eye(N_KV_HEADS, dtype=w.dtype)
    out = jnp.einsum("gh,...ab->...gahb", eye, w)
    return out.reshape(w.shape[:-2] + (N_KV_HEADS * w.shape[-2], N_KV_HEADS * w.shape[-1]))


def _cmp_dist(q_pos):
    return q_pos[:, None] - (jnp.arange(LANES) * CMP_BLOCK + CMP_BLOCK - 1)[None, :]


def _layer(hp, hs_t, cache_l, pt_flat, win_l, pool_l, conv_l, wts, table, dims):
    (g_pre_mix, w_in, pe_k, w1_k, w2_k, pe_v, w1_v, w2_v, w_pool_grp, pool_scale, w_pool_proj, w_nsa_proj,
     w_out, g_post_mix, g_pre_ffn, w_up, conv_w, conv_b, w_down, g_post_ffn) = wts
    batch, seq, bs, ts, n_pages, page = dims
    d_model = hp.shape[1]
    pool_w = w_pool_grp.shape[0] * w_pool_grp.shape[1]
    q_w = N_HEADS * HEAD_DIM
    d_ff = w_down.shape[0]
    past_len = n_pages * page
    c_q, c_kv, c_ng, c_mg = pool_w, pool_w + q_w, pool_w + q_w + 6 * KVD, pool_w + q_w + 6 * KVD + 3 * N_HEADS
    assert pool_w == q_w == 4 * KVD, "column blocks below assume equal widths"

    row = lambda v: v.reshape(1, -1)
    w_a = w_in[:, :c_ng].astype(BF16)
    w_ng = jnp.pad(w_in[:, c_ng:c_mg], ((0, 0), (0, LANES - 3 * N_HEADS))).astype(BF16)
    w_mg = w_in[:, c_mg:].astype(BF16)
    w_grp = w_pool_grp.astype(BF16)
    w1bd = jnp.stack([_block_diag4(w1_k), _block_diag4(w1_v)]).astype(BF16)
    w2bd = jnp.stack([_block_diag4(w2_k), _block_diag4(w2_v)]).astype(BF16)
    chunk = 2 * page
    blk_chunk = chunk // CMP_BLOCK
    regroup = (jnp.arange(chunk) % blk_chunk) * CMP_BLOCK + jnp.arange(chunk) // blk_chunk
    perm = (jnp.arange(chunk)[None, :] == regroup[:, None]).astype(BF16)
    pe_n = jnp.stack([jnp.tile(pe, (blk_chunk, N_KV_HEADS)) for pe in (pe_k, pe_v)])
    pe_t = pe_n.transpose(0, 2, 1)
    w_up16 = w_up.astype(BF16)
    wd = w_down.astype(BF16)
    wpp, wnp, wo = w_pool_proj.astype(BF16), w_nsa_proj.astype(BF16), w_out.astype(BF16)
    heads = table.reshape(N_BUCKETS, N_HEADS)

    z = rms_matmul(hp, row(g_pre_mix), w_a, tm=1024, tn=512, name="proj_main_prompt")
    gates = rms_matmul(hp, row(g_pre_mix), w_mg, tm=1024, tn=512, act="sigmoid", name="proj_gate_prompt")
    ng = rms_matmul(hp, row(g_pre_mix), w_ng, tm=1024, tn=LANES, act="sigmoid", name="proj_ng_prompt")
    a_p = pool_prompt(z, w_grp, row(pool_scale), batch=batch, seq=seq, tt=512)
    z3 = z.reshape(batch, seq, -1)
    kv_rows_p = z3[:, :, c_kv:c_kv + 4 * KVD].reshape(batch, seq, N_PAGED_SLOTS, N_KV_HEADS, HEAD_DIM)
    win_p = z3[:, seq - min(WINDOW, seq):, c_kv + 4 * KVD:c_ng].reshape(batch, -1, 2, N_KV_HEADS, HEAD_DIM)
    pool_new_p = z3[:, seq - POOL_HIST:, :pool_w]

    cmp_p = compress_prompt(z, c_kv // (2 * KVD), perm, pe_n, w1bd, w2bd, batch=batch, seq=seq)
    cmp_p = jnp.pad(cmp_p, ((0, 0), (0, 0), (0, LANES - cmp_p.shape[2]), (0, 0)))
    cmp_p = cmp_p.reshape(batch, 2, LANES, N_KV_HEADS, HEAD_DIM).astype(BF16)
    eye_g = jnp.eye(GROUP, dtype=BF16)
    kc_bd = jnp.einsum("bnkd,gh->bkgdhn", cmp_p[:, 0], eye_g).reshape(batch, N_KV_HEADS, KVD, GROUP * LANES)
    vc_bd = jnp.einsum("bnkd,gh->bkgnhd", cmp_p[:, 1], eye_g).reshape(batch, N_KV_HEADS, GROUP * LANES, KVD)
    pos_p = jnp.arange(seq)
    bias_cmp = _bias_lookup(heads.T, _cmp_dist(pos_p))
    bias_cmp = (bias_cmp.reshape(N_KV_HEADS, GROUP, seq, LANES).transpose(0, 2, 1, 3)
                .reshape(N_KV_HEADS, seq, GROUP * LANES))
    ocmp, sel = cmp_select_prompt(z, c_q // KVD, kc_bd, vc_bd, bias_cmp, batch=batch, seq=seq)
    tok_key = jnp.arange(TQ)[None, :] - jnp.arange(TQ)[:, None]
    near = _bias_lookup(heads.T, jnp.stack([TQ + tok_key, tok_key]))
    far_edge = jnp.where(tok_key <= 0, heads[N_BUCKETS - 1][:, None, None], NEG)
    bias_tile = jnp.stack([far_edge, near[:, 0], jnp.where(tok_key >= 0, near[:, 1], NEG)], axis=1)
    o_p = attn_prompt(heads[N_BUCKETS - 1], z, c_q // KVD, c_kv // KVD + 2, sel, ocmp, ng, bias_tile,
                      batch=batch, seq=seq)
    hp = merge(a_p, o_p, gates, hp, wpp, wnp, wo, row(g_post_mix))
    hp, tails = ffn(hp, None, row(g_pre_ffn), w_up16, conv_w, row(conv_b), wd, row(g_post_ffn),
                    tm=512, tf=512, halo=16, shift=1, tiles_per_seq=seq // 512, tail=8)
    conv_new_p = tails.reshape(batch, seq // 512, 8, d_ff)[:, -1, 8 - (CONV_WIDTH - 1):]

    n_s = ts * bs
    zs = rms_matmul(hs_t, row(g_pre_mix), w_a, tm=n_s, tn=512, name="proj_main_sample")
    gates_s = rms_matmul(hs_t, row(g_pre_mix), w_mg, tm=n_s, tn=512, act="sigmoid", name="proj_gate_sample")
    ng_s = rms_matmul(hs_t, row(g_pre_mix), w_ng, tm=n_s, tn=LANES, act="sigmoid", name="proj_ng_sample")
    hist_t = pool_l.transpose(1, 0, 2)
    a_s = pool_sample(zs, hist_t, w_grp, row(pool_scale), ts=ts, bs=bs, past_len=past_len)
    zs3 = zs.reshape(ts, bs, -1)
    kv_rows_s = zs3[:, :, c_kv:c_kv + 4 * KVD].transpose(1, 0, 2).reshape(bs, ts, N_PAGED_SLOTS, N_KV_HEADS, HEAD_DIM)
    win_rows = zs3[:, :, c_kv + 4 * KVD:c_ng].transpose(1, 0, 2).reshape(bs, ts, 2, N_KV_HEADS, HEAD_DIM)
    win_new_s = jnp.concatenate([win_l, win_rows], axis=1)[:, -win_l.shape[1]:]
    pool_new_s = jnp.concatenate([pool_l, zs3[:, :, :pool_w].transpose(1, 0, 2)], axis=1)[:, -POOL_HIST:]

    cache_t = cache_l.transpose(0, 2, 3, 4, 1).reshape(cache_l.shape[0], N_PAGED_SLOTS * KVD, page)
    cmp_s = compress_sample(cache_t, pt_flat, perm, pe_t, w1bd, w2bd, bs=bs, n_pages=n_pages)
    cmp_s = jnp.pad(cmp_s, ((0, 0), (0, 0), (0, LANES - cmp_s.shape[2]), (0, 0))).astype(BF16)

    rows = GROUP * N_KV_HEADS * ts
    q_s = zs3[:, :, c_q:c_kv].reshape(ts, bs, N_KV_HEADS, GROUP, HEAD_DIM).transpose(1, 3, 2, 0, 4)
    q_bd = jnp.einsum("bgktd,kj->bgktjd", q_s * ATTN_SCALE, jnp.eye(N_KV_HEADS, dtype=F32))
    q_bd = q_bd.reshape(bs, rows, KVD).astype(BF16)
    pad_rows = lambda x: jnp.pad(x.transpose(1, 0, 2), ((0, 0), (0, 16 - ts), (0, 0))).astype(BF16)
    knew = pad_rows(zs3[:, :, c_kv + 2 * KVD:c_kv + 4 * KVD])
    wnew = pad_rows(zs3[:, :, c_kv + 4 * KVD:c_ng])
    wb = win_l.shape[1]
    win_t = win_l.transpose(0, 2, 3, 4, 1).reshape(bs, 2 * KVD, wb)
    r_idx = jnp.arange(rows)
    r_head = ((r_idx % (N_KV_HEADS * ts)) // ts) * GROUP + r_idx // (N_KV_HEADS * ts)
    r_pos = past_len + r_idx % ts
    row_tab = heads.T[r_head]
    rbias = lambda dist: _bias_lookup_rows(row_tab, dist)
    bcmp = rbias(_cmp_dist(r_pos))
    bsel = rbias(r_pos[:, None] - jnp.arange(past_len)[None, :])
    bnew = rbias((r_idx % ts)[:, None] - jnp.arange(16)[None, :])
    bwin = rbias(r_pos[:, None] - (past_len - wb + jnp.arange(wb))[None, :])
    expand = (jnp.arange(LANES)[:, None] == SEL_PER * (jnp.arange(past_len) // SEL_BLOCK)[None, :]).astype(BF16)
    ng_rows = ng_s[:, :3 * N_HEADS].reshape(ts, bs, 3, N_KV_HEADS, GROUP).transpose(1, 4, 3, 0, 2).reshape(bs, rows, 3)
    o_s = attn_sample(cache_t, pt_flat, q_bd, knew, win_t, wnew, cmp_s[:, 0], cmp_s[:, 1], bcmp, bsel, bnew, bwin,
                      bnew, expand, ng_rows, bs=bs, ts=ts, n_pages=n_pages, past_len=past_len)
    o_s = o_s.reshape(bs, GROUP, N_KV_HEADS, ts, HEAD_DIM).transpose(3, 0, 2, 1, 4).reshape(n_s, q_w)
    hs_t = merge(a_s, o_s, gates_s, hs_t, wpp, wnp, wo, row(g_post_mix))
    conv_hist = conv_l.transpose(1, 0, 2).reshape((CONV_WIDTH - 1) * bs, d_ff)
    hs_t, tails_s = ffn(hs_t, conv_hist, row(g_pre_ffn), w_up16, conv_w, row(conv_b), wd, row(g_post_ffn),
                        tm=n_s, tf=512, halo=(CONV_WIDTH - 1) * bs, shift=bs, tiles_per_seq=1,
                        tail=(CONV_WIDTH - 1) * bs)
    conv_new_s = tails_s.reshape(CONV_WIDTH - 1, bs, d_ff).transpose(1, 0, 2)
    return hp, hs_t, (kv_rows_p, kv_rows_s, win_p, win_new_s, pool_new_p, pool_new_s, conv_new_p, conv_new_s)


def kernel(x_prompt, x_sample, cache_kv, page_table, state_kv_win, state_pool, state_conv, g_pre_mix, w_in, pe_cmp_k, w1_cmp_k, w2_cmp_k, pe_cmp_v, w1_cmp_v, w2_cmp_v, rel_bias, w_pool_grp, pool_scale, w_pool_proj, w_nsa_proj, w_out, g_post_mix, g_pre_ffn, w_up, conv_w, conv_b, w_down, g_post_ffn):
    batch, seq, d_model = x_prompt.shape
    bs, ts, _ = x_sample.shape
    depth = cache_kv.shape[0]
    n_pages, page = page_table.shape[1], cache_kv.shape[2]
    assert ts >= CONV_WIDTH - 1
    dims = (batch, seq, bs, ts, n_pages, page)
    pt_flat = page_table.reshape(-1).astype(jnp.int32)
    hp = x_prompt.reshape(batch * seq, d_model)
    hs_t = x_sample.transpose(1, 0, 2).reshape(ts * bs, d_model)
    outs = [[] for _ in range(8)]
    per_layer = (g_pre_mix, w_in, pe_cmp_k, w1_cmp_k, w2_cmp_k, pe_cmp_v, w1_cmp_v, w2_cmp_v, w_pool_grp, pool_scale,
                 w_pool_proj, w_nsa_proj, w_out, g_post_mix, g_pre_ffn, w_up, conv_w, conv_b, w_down, g_post_ffn)
    for l in range(depth):
        wts = tuple(w[l] for w in per_layer)
        hp, hs_t, states = _layer(hp, hs_t, cache_kv[l], pt_flat, state_kv_win[l], state_pool[l], state_conv[l],
                                  wts, rel_bias, dims)
        for acc, s in zip(outs, states):
            acc.append(s)
    y_prompt = hp.reshape(batch, seq, d_model)
    y_sample = hs_t.reshape(ts, bs, d_model).transpose(1, 0, 2)
    return (y_prompt, y_sample) + tuple(jnp.stack(o) for o in outs)
```

```python
import functools
import math

import jax
import jax.numpy as jnp
import numpy as np
from jax import lax
from jax.experimental import pallas as pl
from jax.experimental.pallas import tpu as pltpu

F32 = jnp.float32
BF16 = jnp.bfloat16

EPS = 1e-6
POOL_WINDOWS = (2, 4, 8, 16)
POOL_HIST = max(POOL_WINDOWS) - 1
N_HEADS = 16
HEAD_DIM = 64
N_KV_HEADS = 4
GROUP = N_HEADS // N_KV_HEADS
KVD = N_KV_HEADS * HEAD_DIM
N_PAGED_SLOTS = 4
CMP_BLOCK = 32
SEL_BLOCK = 64
SEL_PER = SEL_BLOCK // CMP_BLOCK
N_SELECT = 16
WINDOW = 512
FORCE_BONUS = 1.0e4
ATTN_SCALE = HEAD_DIM ** -0.5
N_BUCKETS = 32
MAX_DISTANCE = 128
CONV_WIDTH = 3
NEG = -1e30
LANES = 128
TQ = 256
SM_LANES = 128
V_ROWS = HEAD_DIM + 16
CMP_GROUP = 4
SEQ_PER_STEP = 2
FF_PARTS = 2

_NT = (((1,), (1,)), ((), ()))


def _cparams(sem, vmem_mb):
    return pltpu.CompilerParams(dimension_semantics=sem, vmem_limit_bytes=vmem_mb * 1024 * 1024)


def _rms(x, g):
    ms = jnp.mean(x * x, axis=-1, keepdims=True)
    return x * lax.rsqrt(ms + EPS) * g


def _dot(a, b):
    return jnp.dot(a, b, preferred_element_type=F32)


def _dot_nt(a, b):
    return lax.dot_general(a, b, _NT, preferred_element_type=F32)


def _bucket_starts():
    max_exact = N_BUCKETS // 2
    n = np.arange(max_exact, 2 * MAX_DISTANCE)
    x = np.log(n / max_exact) / math.log(MAX_DISTANCE / max_exact) * (N_BUCKETS - max_exact)
    assert np.all(np.abs(x - np.round(x))[(n != max_exact) & (n < MAX_DISTANCE)] > 1e-3)
    large = np.minimum(max_exact + x.astype(np.int64), N_BUCKETS - 1)
    starts = list(range(max_exact)) + [int(n[np.argmax(large >= k)]) for k in range(max_exact, N_BUCKETS)]
    assert starts == sorted(set(starts)) and starts[-1] < MAX_DISTANCE
    return starts


_BUCKET_STARTS = _bucket_starts()


def _bias_lookup(tab, dist):
    col = lambda k: tab[:, k].reshape((-1,) + (1,) * dist.ndim)
    out = jnp.broadcast_to(col(0), (tab.shape[0],) + dist.shape)
    for k in range(1, N_BUCKETS):
        out = jnp.where(dist[None] >= _BUCKET_STARTS[k], col(k), out)
    return out


def _bias_lookup_rows(tab, dist):
    out = jnp.broadcast_to(tab[:, 0:1], dist.shape)
    for k in range(1, N_BUCKETS):
        out = jnp.where(dist >= _BUCKET_STARTS[k], tab[:, k:k + 1], out)
    return out


def _softmax_parts(s, mask):
    s = jnp.where(mask, s, NEG)
    m = jnp.max(s, axis=-1, keepdims=True)
    e = jnp.where(mask, jnp.exp(s - m), 0.0)
    d = jnp.sum(e, axis=-1, keepdims=True)
    return e, d


def _topk_mask(score, n_blocks, k):
    lane = lax.broadcasted_iota(jnp.int32, score.shape, 1)
    rank = jnp.zeros(score.shape, F32)
    for i in range(n_blocks):
        ci = score[:, SEL_PER * i:SEL_PER * i + 1]
        rank = rank + jnp.where(lane > SEL_PER * i, jnp.where(ci >= score, 1.0, 0.0), jnp.where(ci > score, 1.0, 0.0))
    return jnp.where((rank < k) & (lane % SEL_PER == 0) & (lane < SEL_PER * n_blocks), 1.0, 0.0)


def _block_valid(shape, q_pos, n_sb):
    lane = lax.broadcasted_iota(jnp.int32, shape, 1)
    blk = lane // SEL_PER
    return (lane % SEL_PER == 0) & (blk * SEL_BLOCK <= q_pos) & (blk < n_sb)


def _block_scores(p_sum, q_pos, n_sb):
    imp = p_sum
    for k in range(1, SEL_PER):
        imp = imp + pltpu.roll(p_sum, LANES - k, axis=1)
    blk = lax.broadcasted_iota(jnp.int32, imp.shape, 1) // SEL_PER
    cur = q_pos // SEL_BLOCK
    forced = (blk == 0) | (blk == cur) | (blk == cur - 1)
    return jnp.where(_block_valid(imp.shape, q_pos, n_sb), imp + FORCE_BONUS * forced.astype(F32), -jnp.inf)


def _rms_matmul_kernel(x_ref, g_ref, w_ref, o_ref, xn_ref, *, act):
    @pl.when(pl.program_id(1) == 0)
    def _():
        xn_ref[...] = _rms(x_ref[...], g_ref[...]).astype(BF16)

    acc = _dot(xn_ref[...], w_ref[...])
    if act == "sigmoid":
        acc = jax.nn.sigmoid(acc)
    o_ref[...] = acc.astype(o_ref.dtype)


def rms_matmul(x, g, w, *, tm, tn, act=None, name):
    n, k = x.shape
    m = w.shape[1]
    return pl.pallas_call(
        functools.partial(_rms_matmul_kernel, act=act),
        grid=(n // tm, m // tn),
        in_specs=[pl.BlockSpec((tm, k), lambda i, j: (i, 0)),
                  pl.BlockSpec((1, k), lambda i, j: (0, 0)),
                  pl.BlockSpec((k, tn), lambda i, j: (0, j))],
        out_specs=pl.BlockSpec((tm, tn), lambda i, j: (i, j)),
        out_shape=jax.ShapeDtypeStruct((n, m), F32),
        scratch_shapes=[pltpu.VMEM((tm, k), BF16)],
        compiler_params=_cparams(("arbitrary", "arbitrary"), 48),
        name=name,
    )(x, g, w)


def _pool_prompt_kernel(x_ref, h_ref, w_ref, sc_ref, o_ref, buf_ref, *, tt):
    i = pl.program_id(1)
    pg = w_ref.shape[1]
    buf_ref[0:16, :] = jnp.where(i == 0, 0.0, h_ref[...])
    buf_ref[16:, :] = x_ref[...]
    pos = i * tt + lax.broadcasted_iota(jnp.int32, (tt, 1), 0)
    for gi, w in enumerate(POOL_WINDOWS):
        c = slice(gi * pg, (gi + 1) * pg)
        x = buf_ref[16:, c]
        win = x
        for k in range(1, w):
            win = win + buf_ref[16 - k:16 - k + tt, c]
        cnt = jnp.minimum(w, pos + 1).astype(F32)
        pooled = win / cnt - x
        o_ref[:, c] = (_dot(pooled.astype(BF16), w_ref[gi]) * sc_ref[:, c]).astype(o_ref.dtype)


def pool_prompt(z, w_grp, scale, *, batch, seq, tt):
    pw = w_grp.shape[0] * w_grp.shape[1]
    nt = seq // tt
    return pl.pallas_call(
        functools.partial(_pool_prompt_kernel, tt=tt),
        grid=(batch, nt),
        in_specs=[pl.BlockSpec((tt, pw), lambda b, i: (b * nt + i, 0)),
                  pl.BlockSpec((16, pw), lambda b, i: (jnp.maximum((b * nt + i) * (tt // 16) - 1, 0), 0)),
                  pl.BlockSpec(w_grp.shape, lambda b, i: (0, 0, 0)),
                  pl.BlockSpec((1, pw), lambda b, i: (0, 0))],
        out_specs=pl.BlockSpec((tt, pw), lambda b, i: (b * nt + i, 0)),
        out_shape=jax.ShapeDtypeStruct((batch * seq, pw), BF16),
        scratch_shapes=[pltpu.VMEM((tt + 16, pw), F32)],
        compiler_params=_cparams(("arbitrary", "arbitrary"), 32),
        name="pool_prompt",
    )(z, z, w_grp, scale)


def _pool_sample_kernel(x_ref, h_ref, w_ref, sc_ref, o_ref, *, ts, bs, past_len):
    pg = w_ref.shape[1]

    def row(j, c):
        if j < POOL_HIST:
            return h_ref[j, :, c]
        return x_ref[(j - POOL_HIST) * bs:(j - POOL_HIST + 1) * bs, c]

    for t in range(ts):
        for gi, w in enumerate(POOL_WINDOWS):
            c = slice(gi * pg, (gi + 1) * pg)
            x = row(POOL_HIST + t, c)
            win = x
            for k in range(1, w):
                win = win + row(POOL_HIST + t - k, c)
            cnt = float(min(w, past_len + t + 1))
            pooled = win / cnt - x
            o_ref[t * bs:(t + 1) * bs, c] = (_dot(pooled.astype(BF16), w_ref[gi]) * sc_ref[:, c]).astype(o_ref.dtype)


def pool_sample(z, hist_t, w_grp, scale, *, ts, bs, past_len):
    pw = w_grp.shape[0] * w_grp.shape[1]
    return pl.pallas_call(
        functools.partial(_pool_sample_kernel, ts=ts, bs=bs, past_len=past_len),
        grid=(1,),
        in_specs=[pl.BlockSpec((ts * bs, pw), lambda i: (0, 0)),
                  pl.BlockSpec(hist_t.shape, lambda i: (0, 0, 0)),
                  pl.BlockSpec(w_grp.shape, lambda i: (0, 0, 0)),
                  pl.BlockSpec((1, pw), lambda i: (0, 0))],
        out_specs=pl.BlockSpec((ts * bs, pw), lambda i: (0, 0)),
        out_shape=jax.ShapeDtypeStruct((ts * bs, pw), BF16),
        compiler_params=_cparams(("arbitrary",), 48),
        name="pool_sample",
    )(z, hist_t, w_grp, scale)


def _compress_tail(hid, w2_ref):
    hid = jax.nn.gelu(hid, approximate=True)
    return _dot(hid.astype(BF16), w2_ref[...])


def _compress_prompt_kernel(x_ref, perm_ref, pe_ref, w1_ref, w2_ref, o_ref, xs_ref, *, n_blk, chunk):
    n_chunks = x_ref.shape[0] // chunk
    blk_chunk = chunk // CMP_BLOCK
    for s in range(2):
        c = slice(s * KVD, (s + 1) * KVD)
        for ch in range(n_chunks):
            x = (x_ref[ch * chunk:(ch + 1) * chunk, c] + pe_ref[s]).astype(BF16)
            xs_ref[ch] = _dot(perm_ref[...], x).reshape(CMP_BLOCK, blk_chunk, KVD)
        hid = jnp.zeros((n_blk, KVD), F32)
        for l in range(CMP_BLOCK):
            hid = hid + _dot(xs_ref[:, l].reshape(n_blk, KVD).astype(BF16), w1_ref[s, l])
        o_ref[0, s] = _compress_tail(hid, w2_ref.at[s])


def compress_prompt(z, col_block, perm, pe_n, w1bd, w2bd, *, batch, seq):
    n_blk = seq // CMP_BLOCK
    chunk = perm.shape[0]
    assert seq % chunk == 0
    return pl.pallas_call(
        functools.partial(_compress_prompt_kernel, n_blk=n_blk, chunk=chunk),
        grid=(batch,),
        in_specs=[pl.BlockSpec((seq, 2 * KVD), lambda b: (b, col_block)),
                  pl.BlockSpec(perm.shape, lambda b: (0, 0)),
                  pl.BlockSpec(pe_n.shape, lambda b: (0, 0, 0)),
                  pl.BlockSpec(w1bd.shape, lambda b: (0, 0, 0, 0), pipeline_mode=pl.Buffered(1)),
                  pl.BlockSpec(w2bd.shape, lambda b: (0, 0, 0))],
        out_specs=pl.BlockSpec((1, 2, n_blk, KVD), lambda b: (b, 0, 0, 0)),
        out_shape=jax.ShapeDtypeStruct((batch, 2, n_blk, KVD), F32),
        scratch_shapes=[pltpu.VMEM((seq // chunk, CMP_BLOCK, chunk // CMP_BLOCK, KVD), F32)],
        compiler_params=_cparams(("arbitrary",), 48),
        name="compress_prompt",
    )(z, perm, pe_n, w1bd, w2bd)


def _compress_sample_kernel(pt_ref, *refs, n_pages, page):
    del pt_ref
    page_refs = refs[:n_pages]
    perm_ref, pe_ref, w1_ref, w2_ref, o_ref, x_ref = refs[n_pages:]
    b = pl.program_id(0)
    bl = b % CMP_GROUP
    n_pairs = n_pages // 2
    blk_pair = 2 * page // CMP_BLOCK
    for pp in range(n_pairs):
        for s in range(2):
            r = slice(s * KVD, (s + 1) * KVD)
            xt = jnp.concatenate([page_refs[2 * pp][0, r, :], page_refs[2 * pp + 1][0, r, :]], axis=1)
            xt = (xt + pe_ref[s]).astype(BF16)
            xp = _dot_nt(perm_ref[...], xt)
            x_ref[s, bl * n_pairs + pp] = xp.reshape(CMP_BLOCK, blk_pair, KVD)

    @pl.when(bl == CMP_GROUP - 1)
    def _():
        rows = CMP_GROUP * n_pairs * blk_pair
        for s in range(2):
            hid = jnp.zeros((rows, KVD), F32)
            for l in range(CMP_BLOCK):
                xl = x_ref[s, :, l].reshape(rows, KVD)
                hid = hid + _dot(xl.astype(BF16), w1_ref[s, l])
            o_ref[:, s] = _compress_tail(hid, w2_ref.at[s]).reshape(CMP_GROUP, n_pairs * blk_pair, KVD)


def compress_sample(cache_t, pt_flat, perm, pe_t, w1bd, w2bd, *, bs, n_pages):
    page = cache_t.shape[2]
    assert page == LANES and n_pages % 2 == 0 and bs % CMP_GROUP == 0
    n_pairs = n_pages // 2
    blk_pair = 2 * page // CMP_BLOCK
    n_blk = n_pairs * blk_pair

    def page_map(b, pt, *, p):
        return (pt[b * n_pages + p], 0, 0)

    const = lambda nd: (lambda b, pt: (0,) * nd)
    grid_spec = pltpu.PrefetchScalarGridSpec(
        num_scalar_prefetch=1,
        grid=(bs,),
        in_specs=[pl.BlockSpec((1, 2 * KVD, page), functools.partial(page_map, p=p)) for p in range(n_pages)]
        + [pl.BlockSpec(perm.shape, const(2)), pl.BlockSpec(pe_t.shape, const(3)),
           pl.BlockSpec(w1bd.shape, const(4), pipeline_mode=pl.Buffered(1)), pl.BlockSpec(w2bd.shape, const(3))],
        out_specs=pl.BlockSpec((CMP_GROUP, 2, n_blk, KVD), lambda b, pt: (b // CMP_GROUP, 0, 0, 0)),
        scratch_shapes=[pltpu.VMEM((2, CMP_GROUP * n_pairs, CMP_BLOCK, blk_pair, KVD), F32)],
    )
    return pl.pallas_call(
        functools.partial(_compress_sample_kernel, n_pages=n_pages, page=page),
        grid_spec=grid_spec,
        out_shape=jax.ShapeDtypeStruct((bs, 2, n_blk, KVD), F32),
        compiler_params=_cparams(("arbitrary",), 56),
        name="compress_sample",
    )(pt_flat, *([cache_t] * n_pages), perm, pe_t, w1bd, w2bd)


def _cmp_valid(q_pos, shape, n_cmp):
    n = lax.broadcasted_iota(jnp.int32, shape, 1) % LANES
    return (n < n_cmp) & (n * CMP_BLOCK + CMP_BLOCK - 1 <= q_pos)


def _cmp_select_prompt_kernel(q_ref, kc_ref, vc_ref, bias_ref, ocmp_ref, sel_ref, *, n_cmp, n_sb):
    qb = pl.program_id(2)
    q_pos = qb * TQ + lax.broadcasted_iota(jnp.int32, (TQ, 1), 0)
    valid = _cmp_valid(q_pos, (TQ, LANES), n_cmp)
    q = (q_ref[...] * ATTN_SCALE).astype(BF16)
    s = _dot(q, kc_ref[0, 0]) + bias_ref[0]
    p_sum = jnp.zeros((TQ, LANES), F32)
    probs = []
    for g in range(GROUP):
        e, d = _softmax_parts(s[:, g * LANES:(g + 1) * LANES], valid)
        p = e / jnp.where(d > 0, d, 1.0)
        probs.append(p.astype(BF16))
        p_sum = p_sum + p
    ocmp_ref[...] = _dot(jnp.concatenate(probs, axis=1), vc_ref[0, 0])

    k_sel = min(N_SELECT, n_sb)
    srows = SEL_PER * n_sb
    row = lax.broadcasted_iota(jnp.int32, (srows, TQ), 0)
    pos_t = qb * TQ + lax.broadcasted_iota(jnp.int32, (1, TQ), 1)
    blk = row // SEL_PER
    started = (row % SEL_PER == 0) & (blk * SEL_BLOCK <= pos_t)
    all_fit = (qb + 1) * TQ <= k_sel * SEL_BLOCK
    sel_ref[0, 0, 0] = jnp.zeros((LANES, TQ), F32)

    @pl.when(all_fit)
    def _():
        sel_ref[0, 0, 0, 0:srows, :] = jnp.where(started, 1.0, 0.0)

    @pl.when(jnp.logical_not(all_fit))
    def _():
        pt = p_sum.T
        imp = pt[0:srows]
        for k in range(1, SEL_PER):
            imp = imp + pltpu.roll(pt, LANES - k, axis=0)[0:srows]
        cur = pos_t // SEL_BLOCK
        forced = (blk == 0) | (blk == cur) | (blk == cur - 1)
        score = jnp.where(started, imp + FORCE_BONUS * forced.astype(F32), -jnp.inf)
        rank = jnp.zeros((srows, TQ), F32)
        for i in range(n_sb):
            ci = score[SEL_PER * i:SEL_PER * i + 1, :]
            rank = rank + jnp.where(row > SEL_PER * i, jnp.where(ci >= score, 1.0, 0.0), jnp.where(ci > score, 1.0, 0.0))
        sel_ref[0, 0, 0, 0:srows, :] = jnp.where((rank < k_sel) & (row % SEL_PER == 0), 1.0, 0.0)


def cmp_select_prompt(z, q_col, kc_bd, vc_bd, bias_cmp, *, batch, seq):
    n_cmp = seq // CMP_BLOCK
    n_sb = -(-seq // SEL_BLOCK)
    nq = seq // TQ
    gw = GROUP * HEAD_DIM
    assert SEL_PER * n_sb <= LANES and n_cmp <= LANES
    return pl.pallas_call(
        functools.partial(_cmp_select_prompt_kernel, n_cmp=n_cmp, n_sb=n_sb),
        grid=(batch, N_KV_HEADS, nq),
        in_specs=[pl.BlockSpec((TQ, gw), lambda b, kv, i: (b * nq + i, q_col + kv)),
                  pl.BlockSpec((1, 1, gw, GROUP * LANES), lambda b, kv, i: (b, kv, 0, 0)),
                  pl.BlockSpec((1, 1, GROUP * LANES, gw), lambda b, kv, i: (b, kv, 0, 0)),
                  pl.BlockSpec((1, TQ, GROUP * LANES), lambda b, kv, i: (kv, i, 0))],
        out_specs=[pl.BlockSpec((TQ, gw), lambda b, kv, i: (b * nq + i, kv)),
                   pl.BlockSpec((1, 1, 1, LANES, TQ), lambda b, kv, i: (b, kv, i, 0, 0))],
        out_shape=[jax.ShapeDtypeStruct((batch * seq, N_HEADS * HEAD_DIM), F32),
                   jax.ShapeDtypeStruct((batch, N_KV_HEADS, nq, LANES, TQ), F32)],
        compiler_params=_cparams(("arbitrary",) * 3, 32),
        name="cmp_select_prompt",
    )(z, kc_bd, vc_bd, bias_cmp)


def _attn_prompt_kernel(far_ref, q_ref, ks_ref, vs_ref, kw_ref, vw_ref, sel_ref, ocmp_ref, ng_ref, bt_ref, o_ref,
                        qbd_ref, ngt_ref, *state_refs):
    kv = pl.program_id(1)
    qb = pl.program_id(2)
    gw = GROUP * HEAD_DIM
    sel_state, win_state = state_refs[:5], state_refs[5:]
    blk_of_key = lax.broadcasted_iota(jnp.int32, (TQ, LANES), 0) // SEL_BLOCK
    blk_lane = lax.broadcasted_iota(jnp.int32, (TQ, LANES), 1)
    near0 = jnp.maximum(qb - 1, 0)
    eye = (lax.broadcasted_iota(jnp.int32, (gw, gw), 0) == lax.broadcasted_iota(jnp.int32, (gw, gw), 1)).astype(BF16)
    qt = _dot_nt(eye, (q_ref[...] * ATTN_SCALE).astype(BF16)).astype(BF16)
    qbd_ref[0:KVD, :] = jnp.zeros((KVD, GROUP * TQ), BF16)
    row0 = pl.multiple_of(kv * HEAD_DIM, HEAD_DIM)
    sel_neg = ((sel_ref[0, 0, 0] - 1.0) * (-NEG)).astype(BF16)
    for g in range(GROUP):
        qbd_ref[pl.ds(row0, HEAD_DIM), g * TQ:(g + 1) * TQ] = qt[g * HEAD_DIM:(g + 1) * HEAD_DIM, :]
        qbd_ref[KVD:, g * TQ:(g + 1) * TQ] = sel_neg
    vrow = lax.broadcasted_iota(jnp.int32, (V_ROWS, KVD), 0)
    vpick = ((lax.broadcasted_iota(jnp.int32, (V_ROWS, KVD), 1) == kv * HEAD_DIM + vrow) & (vrow < HEAD_DIM))
    vpick = vpick.astype(BF16)
    ones_row = lax.broadcasted_iota(jnp.int32, (V_ROWS, TQ), 0) == HEAD_DIM

    def chunk(state, j, k_ref, v_ref, selected, tile):
        s_ref, p_ref, m_ref, a_ref, acc_ref = state
        start = pl.multiple_of(j * TQ, TQ)
        k = k_ref[pl.ds(start, TQ), :].astype(BF16)
        if selected:
            onehot = (blk_lane == SEL_PER * (blk_of_key + j * (TQ // SEL_BLOCK))).astype(BF16)
            s_ref[...] = _dot(jnp.concatenate([k, onehot], axis=1), qbd_ref[...])
        else:
            s_ref[...] = _dot(k, qbd_ref[0:KVD, :])
        vt = jnp.where(ones_row, 1.0, _dot_nt(vpick, v_ref[pl.ds(start, TQ), :].astype(BF16))).astype(BF16)
        for g in range(GROUP):
            far = far_ref[kv * GROUP + g]
            for half in range(TQ // SM_LANES):
                c = slice(g * TQ + half * SM_LANES, g * TQ + (half + 1) * SM_LANES)
                u = s_ref[:, c]
                if tile is not None:
                    u = u + bt_ref[g, tile, :, half * SM_LANES:(half + 1) * SM_LANES]
                m_old = m_ref[:, c]
                mx = jnp.max(u, axis=0, keepdims=True)
                m_new = jnp.maximum(m_old, mx + far if tile is None else mx)
                a_ref[:, c] = jnp.exp(m_old - m_new)
                m_ref[:, c] = m_new
                p_ref[:, c] = jnp.exp(u - (m_new - far if tile is None else m_new)).astype(BF16)
        acc_ref[...] = a_ref[...] * acc_ref[...] + _dot(vt, p_ref[...])

    def finish(state):
        acc_ref = state[4]
        l = acc_ref[HEAD_DIM:HEAD_DIM + 1, :]
        return acc_ref[0:HEAD_DIM, :] / jnp.where(l > 0, l, 1.0)

    for state in (sel_state, win_state):
        state[2][...] = jnp.full(state[2].shape, NEG, F32)
        state[4][...] = jnp.zeros(state[4].shape, F32)

    def far_body(j, c):
        chunk(sel_state, j, ks_ref, vs_ref, True, None)
        return c
    lax.fori_loop(0, near0, far_body, 0)

    @pl.when(qb >= WINDOW // TQ)
    def _():
        chunk(win_state, qb - WINDOW // TQ, kw_ref, vw_ref, False, 0)

    def near_body(j, c):
        tile = j - (qb - 1) + 1
        chunk(sel_state, j, ks_ref, vs_ref, True, tile)
        chunk(win_state, j, kw_ref, vw_ref, False, tile)
        return c
    lax.fori_loop(near0, qb + 1, near_body, 0)
    o_sel = finish(sel_state)
    o_win = finish(win_state)
    ngt_ref[...] = ng_ref[...].T
    ocmp_t = ocmp_ref[...].T
    outs = []
    for g in range(GROUP):
        gate = lambda r: ngt_ref[pl.ds(r * N_HEADS + kv * GROUP + g, 1), :]
        c = slice(g * TQ, (g + 1) * TQ)
        outs.append(gate(0) * ocmp_t[g * HEAD_DIM:(g + 1) * HEAD_DIM, :] + gate(1) * o_sel[:, c]
                    + gate(2) * o_win[:, c])
    o_t = jnp.concatenate(outs, axis=0).astype(BF16)
    o_ref[...] = _dot_nt(eye, o_t).astype(o_ref.dtype)


def attn_prompt(far, z, q_col, kv_col, sel, ocmp, ng, bias_tile, *, batch, seq):
    assert WINDOW == 2 * TQ and seq % TQ == 0 and TQ % SM_LANES == 0 and TQ == GROUP * HEAD_DIM == KVD
    cols = GROUP * TQ
    nq = seq // TQ
    kv_spec = lambda slot: pl.BlockSpec((seq, KVD), lambda b, kv, i: (b, kv_col + slot))
    tok_spec = lambda col0: pl.BlockSpec((TQ, KVD), lambda b, kv, i: (b * nq + i, col0 + kv))
    return pl.pallas_call(
        _attn_prompt_kernel,
        grid=(batch, N_KV_HEADS, nq),
        in_specs=[pl.BlockSpec(memory_space=pltpu.SMEM),
                  tok_spec(q_col), kv_spec(0), kv_spec(1), kv_spec(2), kv_spec(3),
                  pl.BlockSpec((1, 1, 1, LANES, TQ), lambda b, kv, i: (b, kv, i, 0, 0)),
                  tok_spec(0),
                  pl.BlockSpec((TQ, LANES), lambda b, kv, i: (b * nq + i, 0)),
                  pl.BlockSpec((GROUP, 3, TQ, TQ), lambda b, kv, i: (kv, 0, 0, 0))],
        out_specs=tok_spec(0),
        out_shape=jax.ShapeDtypeStruct((batch * seq, N_HEADS * HEAD_DIM), BF16),
        scratch_shapes=[pltpu.VMEM((KVD + LANES, cols), BF16), pltpu.VMEM((LANES, TQ), F32)]
        + 2 * [pltpu.VMEM((TQ, cols), F32), pltpu.VMEM((TQ, cols), BF16),
               pltpu.VMEM((1, cols), F32), pltpu.VMEM((1, cols), F32), pltpu.VMEM((V_ROWS, cols), F32)],
        compiler_params=_cparams(("arbitrary",) * 3, 48),
        name="attn_prompt",
    )(far, z, z, z, z, z, sel, ocmp, ng, bias_tile)


def _diag_heads(x, ts):
    rows = x.shape[0]
    kv_of_row = (lax.broadcasted_iota(jnp.int32, (rows, HEAD_DIM), 0) % (N_KV_HEADS * ts)) // ts
    out = jnp.zeros((rows, HEAD_DIM), F32)
    for kv in range(N_KV_HEADS):
        out = out + jnp.where(kv_of_row == kv, x[:, kv * HEAD_DIM:(kv + 1) * HEAD_DIM], 0.0)
    return out


def _attn_sample_kernel(pt_ref, *refs, n_pages, page, ts, past_len, wb):
    del pt_ref
    page_refs = refs[:SEQ_PER_STEP * n_pages]
    (q_ref, knew_ref, win_ref, wnew_ref, kc_ref, vc_ref, bcmp_ref, bsel_ref, bseln_ref, bwin_ref, bwinn_ref,
     expand_ref, ng_ref, o_ref, kt_ref, vt_ref) = refs[SEQ_PER_STEP * n_pages:]
    rows = GROUP * N_KV_HEADS * ts
    kvt = N_KV_HEADS * ts
    n_cmp = past_len // CMP_BLOCK
    n_sb = -(-(past_len + ts) // SEL_BLOCK)
    t_of_row = lax.broadcasted_iota(jnp.int32, (rows, 1), 0) % ts
    q_pos = past_len + t_of_row
    new_col = lax.broadcasted_iota(jnp.int32, (rows, 16), 1)
    new_ok = (new_col <= t_of_row) & (new_col < ts)
    new_lane = SEL_PER * (past_len // SEL_BLOCK)
    win_dist = wb + t_of_row - lax.broadcasted_iota(jnp.int32, (rows, wb), 1)
    win_mask = (win_dist >= 0) & (win_dist <= WINDOW)

    for k in range(SEQ_PER_STEP):
        q = q_ref[k]
        s = _dot_nt(q, kc_ref[k]) + bcmp_ref[...]
        e, d = _softmax_parts(s, _cmp_valid(q_pos, (rows, LANES), n_cmp))
        p = e / jnp.where(d > 0, d, 1.0)
        o_cmp = _diag_heads(_dot(p.astype(BF16), vc_ref[k]), ts)
        p_sum = p[0:kvt]
        for g in range(1, GROUP):
            p_sum = p_sum + p[g * kvt:(g + 1) * kvt]
        sel = _topk_mask(_block_scores(p_sum, q_pos[0:kvt], n_sb), n_sb, min(N_SELECT, n_sb))
        sel = jnp.concatenate([sel] * GROUP, axis=0)

        for pg in range(n_pages):
            pref = page_refs[k * n_pages + pg]
            kt_ref[k, :, pg * page:(pg + 1) * page] = pref[0, 0:KVD, :].astype(BF16)
            vt_ref[k, :, pg * page:(pg + 1) * page] = pref[0, KVD:2 * KVD, :].astype(BF16)
        s = _dot(q, kt_ref[k]) + bsel_ref[...]
        mask = _dot(sel.astype(BF16), expand_ref[...]) > 0.5
        sn = _dot_nt(q, knew_ref[k, :, 0:KVD]) + bseln_ref[...]
        mask_n = new_ok & (sel[:, new_lane:new_lane + 1] > 0.5)
        o_sel = _two_part_attention(s, mask, vt_ref[k], sn, mask_n, knew_ref[k, :, KVD:2 * KVD], ts)

        s = _dot(q, win_ref[k, 0:KVD, :].astype(BF16)) + bwin_ref[...]
        sn = _dot_nt(q, wnew_ref[k, :, 0:KVD]) + bwinn_ref[...]
        o_win = _two_part_attention(s, win_mask, win_ref[k, KVD:2 * KVD, :].astype(BF16), sn, new_ok,
                                    wnew_ref[k, :, KVD:2 * KVD], ts)

        ng = ng_ref[k]
        o_ref[k] = (ng[:, 0:1] * o_cmp + ng[:, 1:2] * o_sel + ng[:, 2:3] * o_win).astype(o_ref.dtype)


def _two_part_attention(s, mask, vt, sn, mask_n, vn, ts):
    s = jnp.where(mask, s, NEG)
    sn = jnp.where(mask_n, sn, NEG)
    m = jnp.maximum(jnp.max(s, axis=-1, keepdims=True), jnp.max(sn, axis=-1, keepdims=True))
    e = jnp.where(mask, jnp.exp(s - m), 0.0)
    en = jnp.where(mask_n, jnp.exp(sn - m), 0.0)
    d = jnp.sum(e, axis=-1, keepdims=True) + jnp.sum(en, axis=-1, keepdims=True)
    inv = jnp.where(d > 0, d, 1.0)
    o = _dot_nt((e / inv).astype(BF16), vt) + _dot((en / inv).astype(BF16), vn)
    return _diag_heads(o, ts)


def attn_sample(cache_t, pt_flat, q_bd, knew, win_t, wnew, kc_s, vc_s, bcmp, bsel, bseln, bwin, bwinn, expand, ng_s,
                *, bs, ts, n_pages, past_len):
    page = cache_t.shape[2]
    wb = win_t.shape[2]
    rows = GROUP * N_KV_HEADS * ts
    assert past_len == n_pages * page and past_len % SEL_BLOCK == 0 and ts <= 16
    assert (past_len + ts) // CMP_BLOCK == past_len // CMP_BLOCK
    sps = SEQ_PER_STEP
    assert bs % sps == 0

    def page_map(b, pt, *, p):
        return (pt[b * sps * n_pages + p], 1, 0)

    per_b = lambda shape: pl.BlockSpec((sps,) + shape, lambda b, pt: (b,) + (0,) * len(shape))
    const = lambda a: pl.BlockSpec(a.shape, lambda b, pt: (0,) * a.ndim)
    grid_spec = pltpu.PrefetchScalarGridSpec(
        num_scalar_prefetch=1,
        grid=(bs // sps,),
        in_specs=[pl.BlockSpec((1, 2 * KVD, page), functools.partial(page_map, p=p)) for p in range(sps * n_pages)]
        + [per_b((rows, KVD)), per_b((16, 2 * KVD)), per_b((2 * KVD, wb)), per_b((16, 2 * KVD)),
           per_b((LANES, KVD)), per_b((LANES, KVD)),
           const(bcmp), const(bsel), const(bseln), const(bwin), const(bwinn), const(expand),
           per_b((rows, 3))],
        out_specs=per_b((rows, HEAD_DIM)),
        scratch_shapes=[pltpu.VMEM((sps, KVD, past_len), BF16), pltpu.VMEM((sps, KVD, past_len), BF16)],
    )
    return pl.pallas_call(
        functools.partial(_attn_sample_kernel, n_pages=n_pages, page=page, ts=ts, past_len=past_len, wb=wb),
        grid_spec=grid_spec,
        out_shape=jax.ShapeDtypeStruct((bs, rows, HEAD_DIM), BF16),
        compiler_params=_cparams(("arbitrary",), 48),
        name="attn_sample",
    )(pt_flat, *([cache_t] * (sps * n_pages)), q_bd, knew, win_t, wnew, kc_s, vc_s, bcmp, bsel, bseln, bwin, bwinn,
      expand, ng_s)


def _merge_kernel(a_ref, o_ref, g0_ref, g1_ref, h_ref, wp_ref, wn_ref, wo_ref, gp_ref, out_ref):
    merged = g0_ref[...] * _dot(a_ref[...], wp_ref[...]) + g1_ref[...] * _dot(o_ref[...], wn_ref[...])
    r = _dot(merged.astype(BF16), wo_ref[...])
    out_ref[...] = h_ref[...] + _rms(r, gp_ref[...])


def merge(a, o, gates, h, wp, wn, wo, gp):
    n, d = h.shape
    tm = min(256, n)
    pw = a.shape[1]
    qw = o.shape[1]
    once = pl.Buffered(1)
    return pl.pallas_call(
        _merge_kernel,
        grid=(n // tm,),
        in_specs=[pl.BlockSpec((tm, pw), lambda i: (i, 0)),
                  pl.BlockSpec((tm, qw), lambda i: (i, 0)),
                  pl.BlockSpec((tm, d), lambda i: (i, 0)),
                  pl.BlockSpec((tm, d), lambda i: (i, 1)),
                  pl.BlockSpec((tm, d), lambda i: (i, 0)),
                  pl.BlockSpec(wp.shape, lambda i: (0, 0), pipeline_mode=once),
                  pl.BlockSpec(wn.shape, lambda i: (0, 0), pipeline_mode=once),
                  pl.BlockSpec(wo.shape, lambda i: (0, 0), pipeline_mode=once),
                  pl.BlockSpec((1, d), lambda i: (0, 0))],
        out_specs=pl.BlockSpec((tm, d), lambda i: (i, 0)),
        out_shape=jax.ShapeDtypeStruct((n, d), F32),
        compiler_params=_cparams(("arbitrary",), 56),
        name="merge",
    )(a, o, gates, gates, h, wp, wn, wo, gp)


def _ffn_kernel(h_ref, halo_ref, gpre_ref, wv_ref, wg_ref, cw_ref, cb_ref, wd_ref, gpost_ref,
                out_ref, tail_ref, xn_ref, acc_ref, *, tm, halo, shift, tiles_per_seq, tail, from_rows):
    i = pl.program_id(0)
    f = pl.program_id(1)

    @pl.when(f == 0)
    def _():
        xn_ref[halo:, :] = _rms(h_ref[...], gpre_ref[...]).astype(BF16)
        if from_rows:
            xn_ref[0:halo, :] = _rms(halo_ref[...], gpre_ref[...]).astype(BF16)
        acc_ref[...] = jnp.zeros_like(acc_ref)

    tf = wv_ref.shape[1]
    parts = [slice(k * tf // FF_PARTS, (k + 1) * tf // FF_PARTS) for k in range(FF_PARTS)]
    ups = []
    for c in parts:
        val = _dot(xn_ref[halo:, :], wv_ref[:, c])
        if from_rows:
            gcat = _dot(xn_ref[...], wg_ref[:, c])
            is_hist = lax.broadcasted_iota(jnp.int32, gcat.shape, 0) < halo
            gcat = jnp.where(is_hist & (i % tiles_per_seq == 0), 0.0, gcat)
        else:
            gcat = jnp.concatenate([halo_ref[:, c], _dot(xn_ref[halo:, :], wg_ref[:, c])], axis=0)
        ups.append((val, gcat))
    down = None
    for c, (val, gcat) in zip(parts, ups):
        conv = cb_ref[:, c]
        for j in range(CONV_WIDTH):
            back = (CONV_WIDTH - 1 - j) * shift
            if back % 8 == 0:
                tap = gcat[halo - back:halo - back + tm]
            else:
                tap = pltpu.roll(gcat, back, axis=0)[halo:]
            conv = conv + cw_ref[j:j + 1, c] * tap
        act = jax.nn.gelu(conv, approximate=True) * val
        part = _dot(act.astype(BF16), wd_ref[c, :])
        down = part if down is None else down + part
        tail_ref[:, c] = gcat[halo + tm - tail:, :]
    acc_ref[...] += down

    @pl.when(f == pl.num_programs(1) - 1)
    def _():
        out_ref[...] = h_ref[...] + _rms(acc_ref[...], gpost_ref[...])


def ffn(h, halo_src, gpre, w_up, cw, cb, wd, gpost, *, tm, tf, halo, shift, tiles_per_seq, tail):
    n, d = h.shape
    dff = wd.shape[0]
    nt = n // tm
    assert dff % tf == 0
    from_rows = halo_src is None
    if from_rows:
        halo_src = h
        halo_spec = pl.BlockSpec((halo, d), lambda i, f: (jnp.maximum(i * (tm // halo) - 1, 0), 0))
    else:
        assert nt == 1
        halo_spec = pl.BlockSpec((halo, tf), lambda i, f: (0, f))
    return pl.pallas_call(
        functools.partial(_ffn_kernel, tm=tm, halo=halo, shift=shift, tiles_per_seq=tiles_per_seq, tail=tail,
                          from_rows=from_rows),
        grid=(nt, dff // tf),
        in_specs=[pl.BlockSpec((tm, d), lambda i, f: (i, 0)),
                  halo_spec,
                  pl.BlockSpec((1, d), lambda i, f: (0, 0)),
                  pl.BlockSpec((d, tf), lambda i, f: (0, f)),
                  pl.BlockSpec((d, tf), lambda i, f: (0, dff // tf + f)),
                  pl.BlockSpec((CONV_WIDTH, tf), lambda i, f: (0, f)),
                  pl.BlockSpec((1, tf), lambda i, f: (0, f)),
                  pl.BlockSpec((tf, d), lambda i, f: (f, 0)),
                  pl.BlockSpec((1, d), lambda i, f: (0, 0))],
        out_specs=[pl.BlockSpec((tm, d), lambda i, f: (i, 0)),
                   pl.BlockSpec((tail, tf), lambda i, f: (i, f))],
        out_shape=[jax.ShapeDtypeStruct((n, d), F32),
                   jax.ShapeDtypeStruct((nt * tail, dff), F32)],
        scratch_shapes=[pltpu.VMEM((tm + halo, d), BF16), pltpu.VMEM((tm, d), F32)],
        compiler_params=_cparams(("arbitrary", "arbitrary"), 56),
        name="ffn",
    )(h, halo_src, gpre, w_up, w_up, cw, cb, wd, gpost)


def _block_diag4(w):
    eye = jnp.eye(N_KV_HEADS, dtype=w.dtype)
    out = jnp.einsum("gh,...ab->...gahb", eye, w)
    return out.reshape(w.shape[:-2] + (N_KV_HEADS * w.shape[-2], N_KV_HEADS * w.shape[-1]))


def _cmp_dist(q_pos):
    return q_pos[:, None] - (jnp.arange(LANES) * CMP_BLOCK + CMP_BLOCK - 1)[None, :]


def _layer(hp, hs_t, cache_l, pt_flat, win_l, pool_l, conv_l, wts, table, dims):
    (g_pre_mix, w_in, pe_k, w1_k, w2_k, pe_v, w1_v, w2_v, w_pool_grp, pool_scale, w_pool_proj, w_nsa_proj,
     w_out, g_post_mix, g_pre_ffn, w_up, conv_w, conv_b, w_down, g_post_ffn) = wts
    batch, seq, bs, ts, n_pages, page = dims
    d_model = hp.shape[1]
    pool_w = w_pool_grp.shape[0] * w_pool_grp.shape[1]
    q_w = N_HEADS * HEAD_DIM
    d_ff = w_down.shape[0]
    past_len = n_pages * page
    c_q, c_kv, c_ng, c_mg = pool_w, pool_w + q_w, pool_w + q_w + 6 * KVD, pool_w + q_w + 6 * KVD + 3 * N_HEADS
    assert pool_w == q_w == 4 * KVD, "column blocks below assume equal widths"

    row = lambda v: v.reshape(1, -1)
    w_a = w_in[:, :c_ng].astype(BF16)
    w_ng = jnp.pad(w_in[:, c_ng:c_mg], ((0, 0), (0, LANES - 3 * N_HEADS))).astype(BF16)
    w_mg = w_in[:, c_mg:].astype(BF16)
    w_grp = w_pool_grp.astype(BF16)
    w1bd = jnp.stack([_block_diag4(w1_k), _block_diag4(w1_v)]).astype(BF16)
    w2bd = jnp.stack([_block_diag4(w2_k), _block_diag4(w2_v)]).astype(BF16)
    chunk = 2 * page
    blk_chunk = chunk // CMP_BLOCK
    regroup = (jnp.arange(chunk) % blk_chunk) * CMP_BLOCK + jnp.arange(chunk) // blk_chunk
    perm = (jnp.arange(chunk)[None, :] == regroup[:, None]).astype(BF16)
    pe_n = jnp.stack([jnp.tile(pe, (blk_chunk, N_KV_HEADS)) for pe in (pe_k, pe_v)])
    pe_t = pe_n.transpose(0, 2, 1)
    w_up16 = w_up.astype(BF16)
    wd = w_down.astype(BF16)
    wpp, wnp, wo = w_pool_proj.astype(BF16), w_nsa_proj.astype(BF16), w_out.astype(BF16)
    heads = table.reshape(N_BUCKETS, N_HEADS)

    z = rms_matmul(hp, row(g_pre_mix), w_a, tm=1024, tn=512, name="proj_main_prompt")
    gates = rms_matmul(hp, row(g_pre_mix), w_mg, tm=1024, tn=512, act="sigmoid", name="proj_gate_prompt")
    ng = rms_matmul(hp, row(g_pre_mix), w_ng, tm=1024, tn=LANES, act="sigmoid", name="proj_ng_prompt")
    a_p = pool_prompt(z, w_grp, row(pool_scale), batch=batch, seq=seq, tt=512)
    z3 = z.reshape(batch, seq, -1)
    kv_rows_p = z3[:, :, c_kv:c_kv + 4 * KVD].reshape(batch, seq, N_PAGED_SLOTS, N_KV_HEADS, HEAD_DIM)
    win_p = z3[:, seq - min(WINDOW, seq):, c_kv + 4 * KVD:c_ng].reshape(batch, -1, 2, N_KV_HEADS, HEAD_DIM)
    pool_new_p = z3[:, seq - POOL_HIST:, :pool_w]

    cmp_p = compress_prompt(z, c_kv // (2 * KVD), perm, pe_n, w1bd, w2bd, batch=batch, seq=seq)
    cmp_p = jnp.pad(cmp_p, ((0, 0), (0, 0), (0, LANES - cmp_p.shape[2]), (0, 0)))
    cmp_p = cmp_p.reshape(batch, 2, LANES, N_KV_HEADS, HEAD_DIM).astype(BF16)
    eye_g = jnp.eye(GROUP, dtype=BF16)
    kc_bd = jnp.einsum("bnkd,gh->bkgdhn", cmp_p[:, 0], eye_g).reshape(batch, N_KV_HEADS, KVD, GROUP * LANES)
    vc_bd = jnp.einsum("bnkd,gh->bkgnhd", cmp_p[:, 1], eye_g).reshape(batch, N_KV_HEADS, GROUP * LANES, KVD)
    pos_p = jnp.arange(seq)
    bias_cmp = _bias_lookup(heads.T, _cmp_dist(pos_p))
    bias_cmp = (bias_cmp.reshape(N_KV_HEADS, GROUP, seq, LANES).transpose(0, 2, 1, 3)
                .reshape(N_KV_HEADS, seq, GROUP * LANES))
    ocmp, sel = cmp_select_prompt(z, c_q // KVD, kc_bd, vc_bd, bias_cmp, batch=batch, seq=seq)
    tok_key = jnp.arange(TQ)[None, :] - jnp.arange(TQ)[:, None]
    near = _bias_lookup(heads.T, jnp.stack([TQ + tok_key, tok_key]))
    far_edge = jnp.where(tok_key <= 0, heads[N_BUCKETS - 1][:, None, None], NEG)
    bias_tile = jnp.stack([far_edge, near[:, 0], jnp.where(tok_key >= 0, near[:, 1], NEG)], axis=1)
    o_p = attn_prompt(heads[N_BUCKETS - 1], z, c_q // KVD, c_kv // KVD + 2, sel, ocmp, ng, bias_tile,
                      batch=batch, seq=seq)
    hp = merge(a_p, o_p, gates, hp, wpp, wnp, wo, row(g_post_mix))
    hp, tails = ffn(hp, None, row(g_pre_ffn), w_up16, conv_w, row(conv_b), wd, row(g_post_ffn),
                    tm=512, tf=512, halo=16, shift=1, tiles_per_seq=seq // 512, tail=8)
    conv_new_p = tails.reshape(batch, seq // 512, 8, d_ff)[:, -1, 8 - (CONV_WIDTH - 1):]

    n_s = ts * bs
    zs = rms_matmul(hs_t, row(g_pre_mix), w_a, tm=n_s, tn=512, name="proj_main_sample")
    gates_s = rms_matmul(hs_t, row(g_pre_mix), w_mg, tm=n_s, tn=512, act="sigmoid", name="proj_gate_sample")
    ng_s = rms_matmul(hs_t, row(g_pre_mix), w_ng, tm=n_s, tn=LANES, act="sigmoid", name="proj_ng_sample")
    hist_t = pool_l.transpose(1, 0, 2)
    a_s = pool_sample(zs, hist_t, w_grp, row(pool_scale), ts=ts, bs=bs, past_len=past_len)
    zs3 = zs.reshape(ts, bs, -1)
    kv_rows_s = zs3[:, :, c_kv:c_kv + 4 * KVD].transpose(1, 0, 2).reshape(bs, ts, N_PAGED_SLOTS, N_KV_HEADS, HEAD_DIM)
    win_rows = zs3[:, :, c_kv + 4 * KVD:c_ng].transpose(1, 0, 2).reshape(bs, ts, 2, N_KV_HEADS, HEAD_DIM)
    win_new_s = jnp.concatenate([win_l, win_rows], axis=1)[:, -win_l.shape[1]:]
    pool_new_s = jnp.concatenate([pool_l, zs3[:, :, :pool_w].transpose(1, 0, 2)], axis=1)[:, -POOL_HIST:]

    cache_t = cache_l.transpose(0, 2, 3, 4, 1).reshape(cache_l.shape[0], N_PAGED_SLOTS * KVD, page)
    cmp_s = compress_sample(cache_t, pt_flat, perm, pe_t, w1bd, w2bd, bs=bs, n_pages=n_pages)
    cmp_s = jnp.pad(cmp_s, ((0, 0), (0, 0), (0, LANES - cmp_s.shape[2]), (0, 0))).astype(BF16)

    rows = GROUP * N_KV_HEADS * ts
    q_s = zs3[:, :, c_q:c_kv].reshape(ts, bs, N_KV_HEADS, GROUP, HEAD_DIM).transpose(1, 3, 2, 0, 4)
    q_bd = jnp.einsum("bgktd,kj->bgktjd", q_s * ATTN_SCALE, jnp.eye(N_KV_HEADS, dtype=F32))
    q_bd = q_bd.reshape(bs, rows, KVD).astype(BF16)
    pad_rows = lambda x: jnp.pad(x.transpose(1, 0, 2), ((0, 0), (0, 16 - ts), (0, 0))).astype(BF16)
    knew = pad_rows(zs3[:, :, c_kv + 2 * KVD:c_kv + 4 * KVD])
    wnew = pad_rows(zs3[:, :, c_kv + 4 * KVD:c_ng])
    wb = win_l.shape[1]
    win_t = win_l.transpose(0, 2, 3, 4, 1).reshape(bs, 2 * KVD, wb)
    r_idx = jnp.arange(rows)
    r_head = ((r_idx % (N_KV_HEADS * ts)) // ts) * GROUP + r_idx // (N_KV_HEADS * ts)
    r_pos = past_len + r_idx % ts
    row_tab = heads.T[r_head]
    rbias = lambda dist: _bias_lookup_rows(row_tab, dist)
    bcmp = rbias(_cmp_dist(r_pos))
    bsel = rbias(r_pos[:, None] - jnp.arange(past_len)[None, :])
    bnew = rbias((r_idx % ts)[:, None] - jnp.arange(16)[None, :])
    bwin = rbias(r_pos[:, None] - (past_len - wb + jnp.arange(wb))[None, :])
    expand = (jnp.arange(LANES)[:, None] == SEL_PER * (jnp.arange(past_len) // SEL_BLOCK)[None, :]).astype(BF16)
    ng_rows = ng_s[:, :3 * N_HEADS].reshape(ts, bs, 3, N_KV_HEADS, GROUP).transpose(1, 4, 3, 0, 2).reshape(bs, rows, 3)
    o_s = attn_sample(cache_t, pt_flat, q_bd, knew, win_t, wnew, cmp_s[:, 0], cmp_s[:, 1], bcmp, bsel, bnew, bwin,
                      bnew, expand, ng_rows, bs=bs, ts=ts, n_pages=n_pages, past_len=past_len)
    o_s = o_s.reshape(bs, GROUP, N_KV_HEADS, ts, HEAD_DIM).transpose(3, 0, 2, 1, 4).reshape(n_s, q_w)
    hs_t = merge(a_s, o_s, gates_s, hs_t, wpp, wnp, wo, row(g_post_mix))
    conv_hist = conv_l.transpose(1, 0, 2).reshape((CONV_WIDTH - 1) * bs, d_ff)
    hs_t, tails_s = ffn(hs_t, conv_hist, row(g_pre_ffn), w_up16, conv_w, row(conv_b), wd, row(g_post_ffn),
                        tm=n_s, tf=512, halo=(CONV_WIDTH - 1) * bs, shift=bs, tiles_per_seq=1,
                        tail=(CONV_WIDTH - 1) * bs)
    conv_new_s = tails_s.reshape(CONV_WIDTH - 1, bs, d_ff).transpose(1, 0, 2)
    return hp, hs_t, (kv_rows_p, kv_rows_s, win_p, win_new_s, pool_new_p, pool_new_s, conv_new_p, conv_new_s)


def kernel(x_prompt, x_sample, cache_kv, page_table, state_kv_win, state_pool, state_conv, g_pre_mix, w_in, pe_cmp_k, w1_cmp_k, w2_cmp_k, pe_cmp_v, w1_cmp_v, w2_cmp_v, rel_bias, w_pool_grp, pool_scale, w_pool_proj, w_nsa_proj, w_out, g_post_mix, g_pre_ffn, w_up, conv_w, conv_b, w_down, g_post_ffn):
    batch, seq, d_model = x_prompt.shape
    bs, ts, _ = x_sample.shape
    depth = cache_kv.shape[0]
    n_pages, page = page_table.shape[1], cache_kv.shape[2]
    assert ts >= CONV_WIDTH - 1
    dims = (batch, seq, bs, ts, n_pages, page)
    pt_flat = page_table.reshape(-1).astype(jnp.int32)
    hp = x_prompt.reshape(batch * seq, d_model)
    hs_t = x_sample.transpose(1, 0, 2).reshape(ts * bs, d_model)
    outs = [[] for _ in range(8)]
    per_layer = (g_pre_mix, w_in, pe_cmp_k, w1_cmp_k, w2_cmp_k, pe_cmp_v, w1_cmp_v, w2_cmp_v, w_pool_grp, pool_scale,
                 w_pool_proj, w_nsa_proj, w_out, g_post_mix, g_pre_ffn, w_up, conv_w, conv_b, w_down, g_post_ffn)
    for l in range(depth):
        wts = tuple(w[l] for w in per_layer)
        hp, hs_t, states = _layer(hp, hs_t, cache_kv[l], pt_flat, state_kv_win[l], state_pool[l], state_conv[l],
                                  wts, rel_bias, dims)
        for acc, s in zip(outs, states):
            acc.append(s)
    y_prompt = hp.reshape(batch, seq, d_model)
    y_sample = hs_t.reshape(ts, bs, d_model).transpose(1, 0, 2)
    return (y_prompt, y_sample) + tuple(jnp.stack(o) for o in outs)
```

```python
import functools
import math

import jax
import jax.numpy as jnp
import numpy as np
from jax import lax
from jax.experimental import pallas as pl
from jax.experimental.pallas import tpu as pltpu

F32 = jnp.float32
BF16 = jnp.bfloat16

EPS = 1e-6
POOL_WINDOWS = (2, 4, 8, 16)
POOL_HIST = max(POOL_WINDOWS) - 1
N_HEADS = 16
HEAD_DIM = 64
N_KV_HEADS = 4
GROUP = N_HEADS // N_KV_HEADS
KVD = N_KV_HEADS * HEAD_DIM
N_PAGED_SLOTS = 4
CMP_BLOCK = 32
SEL_BLOCK = 64
SEL_PER = SEL_BLOCK // CMP_BLOCK
N_SELECT = 16
WINDOW = 512
FORCE_BONUS = 1.0e4
ATTN_SCALE = HEAD_DIM ** -0.5
N_BUCKETS = 32
MAX_DISTANCE = 128
CONV_WIDTH = 3
NEG = -1e30
LANES = 128
TQ = 256
SM_LANES = 128
V_ROWS = HEAD_DIM + 16
CMP_GROUP = 4
SEQ_PER_STEP = 2
FF_PARTS = 2

_NT = (((1,), (1,)), ((), ()))


def _cparams(sem, vmem_mb):
    return pltpu.CompilerParams(dimension_semantics=sem, vmem_limit_bytes=vmem_mb * 1024 * 1024)


def _rms(x, g):
    ms = jnp.mean(x * x, axis=-1, keepdims=True)
    return x * lax.rsqrt(ms + EPS) * g


def _dot(a, b):
    return jnp.dot(a, b, preferred_element_type=F32)


def _dot_nt(a, b):
    return lax.dot_general(a, b, _NT, preferred_element_type=F32)


def _bucket_starts():
    max_exact = N_BUCKETS // 2
    n = np.arange(max_exact, 2 * MAX_DISTANCE)
    x = np.log(n / max_exact) / math.log(MAX_DISTANCE / max_exact) * (N_BUCKETS - max_exact)
    assert np.all(np.abs(x - np.round(x))[(n != max_exact) & (n < MAX_DISTANCE)] > 1e-3)
    large = np.minimum(max_exact + x.astype(np.int64), N_BUCKETS - 1)
    starts = list(range(max_exact)) + [int(n[np.argmax(large >= k)]) for k in range(max_exact, N_BUCKETS)]
    assert starts == sorted(set(starts)) and starts[-1] < MAX_DISTANCE
    return starts


_BUCKET_STARTS = _bucket_starts()


def _bias_lookup(tab, dist):
    col = lambda k: tab[:, k].reshape((-1,) + (1,) * dist.ndim)
    out = jnp.broadcast_to(col(0), (tab.shape[0],) + dist.shape)
    for k in range(1, N_BUCKETS):
        out = jnp.where(dist[None] >= _BUCKET_STARTS[k], col(k), out)
    return out


def _bias_lookup_rows(tab, dist):
    out = jnp.broadcast_to(tab[:, 0:1], dist.shape)
    for k in range(1, N_BUCKETS):
        out = jnp.where(dist >= _BUCKET_STARTS[k], tab[:, k:k + 1], out)
    return out


def _softmax_parts(s, mask):
    s = jnp.where(mask, s, NEG)
    m = jnp.max(s, axis=-1, keepdims=True)
    e = jnp.where(mask, jnp.exp(s - m), 0.0)
    d = jnp.sum(e, axis=-1, keepdims=True)
    return e, d


def _topk_mask(score, n_blocks, k):
    lane = lax.broadcasted_iota(jnp.int32, score.shape, 1)
    rank = jnp.zeros(score.shape, F32)
    for i in range(n_blocks):
        ci = score[:, SEL_PER * i:SEL_PER * i + 1]
        rank = rank + jnp.where(lane > SEL_PER * i, jnp.where(ci >= score, 1.0, 0.0), jnp.where(ci > score, 1.0, 0.0))
    return jnp.where((rank < k) & (lane % SEL_PER == 0) & (lane < SEL_PER * n_blocks), 1.0, 0.0)


def _block_valid(shape, q_pos, n_sb):
    lane = lax.broadcasted_iota(jnp.int32, shape, 1)
    blk = lane // SEL_PER
    return (lane % SEL_PER == 0) & (blk * SEL_BLOCK <= q_pos) & (blk < n_sb)


def _block_scores(p_sum, q_pos, n_sb):
    imp = p_sum
    for k in range(1, SEL_PER):
        imp = imp + pltpu.roll(p_sum, LANES - k, axis=1)
    blk = lax.broadcasted_iota(jnp.int32, imp.shape, 1) // SEL_PER
    cur = q_pos // SEL_BLOCK
    forced = (blk == 0) | (blk == cur) | (blk == cur - 1)
    return jnp.where(_block_valid(imp.shape, q_pos, n_sb), imp + FORCE_BONUS * forced.astype(F32), -jnp.inf)


def _rms_matmul_kernel(x_ref, g_ref, w_ref, o_ref, xn_ref, *, act):
    @pl.when(pl.program_id(1) == 0)
    def _():
        xn_ref[...] = _rms(x_ref[...], g_ref[...]).astype(BF16)

    acc = _dot(xn_ref[...], w_ref[...])
    if act == "sigmoid":
        acc = jax.nn.sigmoid(acc)
    o_ref[...] = acc.astype(o_ref.dtype)


def rms_matmul(x, g, w, *, tm, tn, act=None, name):
    n, k = x.shape
    m = w.shape[1]
    return pl.pallas_call(
        functools.partial(_rms_matmul_kernel, act=act),
        grid=(n // tm, m // tn),
        in_specs=[pl.BlockSpec((tm, k), lambda i, j: (i, 0)),
                  pl.BlockSpec((1, k), lambda i, j: (0, 0)),
                  pl.BlockSpec((k, tn), lambda i, j: (0, j))],
        out_specs=pl.BlockSpec((tm, tn), lambda i, j: (i, j)),
        out_shape=jax.ShapeDtypeStruct((n, m), F32),
        scratch_shapes=[pltpu.VMEM((tm, k), BF16)],
        compiler_params=_cparams(("arbitrary", "arbitrary"), 48),
        name=name,
    )(x, g, w)


def _pool_prompt_kernel(x_ref, h_ref, w_ref, sc_ref, o_ref, buf_ref, *, tt):
    i = pl.program_id(1)
    pg = w_ref.shape[1]
    buf_ref[0:16, :] = jnp.where(i == 0, 0.0, h_ref[...])
    buf_ref[16:, :] = x_ref[...]
    pos = i * tt + lax.broadcasted_iota(jnp.int32, (tt, 1), 0)
    for gi, w in enumerate(POOL_WINDOWS):
        c = slice(gi * pg, (gi + 1) * pg)
        x = buf_ref[16:, c]
        win = x
        for k in range(1, w):
            win = win + buf_ref[16 - k:16 - k + tt, c]
        cnt = jnp.minimum(w, pos + 1).astype(F32)
        pooled = win / cnt - x
        o_ref[:, c] = (_dot(pooled.astype(BF16), w_ref[gi]) * sc_ref[:, c]).astype(o_ref.dtype)


def pool_prompt(z, w_grp, scale, *, batch, seq, tt):
    pw = w_grp.shape[0] * w_grp.shape[1]
    nt = seq // tt
    return pl.pallas_call(
        functools.partial(_pool_prompt_kernel, tt=tt),
        grid=(batch, nt),
        in_specs=[pl.BlockSpec((tt, pw), lambda b, i: (b * nt + i, 0)),
                  pl.BlockSpec((16, pw), lambda b, i: (jnp.maximum((b * nt + i) * (tt // 16) - 1, 0), 0)),
                  pl.BlockSpec(w_grp.shape, lambda b, i: (0, 0, 0)),
                  pl.BlockSpec((1, pw), lambda b, i: (0, 0))],
        out_specs=pl.BlockSpec((tt, pw), lambda b, i: (b * nt + i, 0)),
        out_shape=jax.ShapeDtypeStruct((batch * seq, pw), BF16),
        scratch_shapes=[pltpu.VMEM((tt + 16, pw), F32)],
        compiler_params=_cparams(("arbitrary", "arbitrary"), 32),
        name="pool_prompt",
    )(z, z, w_grp, scale)


def _pool_sample_kernel(x_ref, h_ref, w_ref, sc_ref, o_ref, *, ts, bs, past_len):
    pg = w_ref.shape[1]

    def row(j, c):
        if j < POOL_HIST:
            return h_ref[j, :, c]
        return x_ref[(j - POOL_HIST) * bs:(j - POOL_HIST + 1) * bs, c]

    for t in range(ts):
        for gi, w in enumerate(POOL_WINDOWS):
            c = slice(gi * pg, (gi + 1) * pg)
            x = row(POOL_HIST + t, c)
            win = x
            for k in range(1, w):
                win = win + row(POOL_HIST + t - k, c)
            cnt = float(min(w, past_len + t + 1))
            pooled = win / cnt - x
            o_ref[t * bs:(t + 1) * bs, c] = (_dot(pooled.astype(BF16), w_ref[gi]) * sc_ref[:, c]).astype(o_ref.dtype)


def pool_sample(z, hist_t, w_grp, scale, *, ts, bs, past_len):
    pw = w_grp.shape[0] * w_grp.shape[1]
    return pl.pallas_call(
        functools.partial(_pool_sample_kernel, ts=ts, bs=bs, past_len=past_len),
        grid=(1,),
        in_specs=[pl.BlockSpec((ts * bs, pw), lambda i: (0, 0)),
                  pl.BlockSpec(hist_t.shape, lambda i: (0, 0, 0)),
                  pl.BlockSpec(w_grp.shape, lambda i: (0, 0, 0)),
                  pl.BlockSpec((1, pw), lambda i: (0, 0))],
        out_specs=pl.BlockSpec((ts * bs, pw), lambda i: (0, 0)),
        out_shape=jax.ShapeDtypeStruct((ts * bs, pw), BF16),
        compiler_params=_cparams(("arbitrary",), 48),
        name="pool_sample",
    )(z, hist_t, w_grp, scale)


def _compress_tail(hid, w2_ref):
    hid = jax.nn.gelu(hid, approximate=True)
    return _dot(hid.astype(BF16), w2_ref[...])


def _compress_prompt_kernel(x_ref, perm_ref, pe_ref, w1_ref, w2_ref, o_ref, xs_ref, *, n_blk, chunk):
    n_chunks = x_ref.shape[0] // chunk
    blk_chunk = chunk // CMP_BLOCK
    for s in range(2):
        c = slice(s * KVD, (s + 1) * KVD)
        for ch in range(n_chunks):
            x = (x_ref[ch * chunk:(ch + 1) * chunk, c] + pe_ref[s]).astype(BF16)
            xs_ref[ch] = _dot(perm_ref[...], x).reshape(CMP_BLOCK, blk_chunk, KVD)
        hid = jnp.zeros((n_blk, KVD), F32)
        for l in range(CMP_BLOCK):
            hid = hid + _dot(xs_ref[:, l].reshape(n_blk, KVD).astype(BF16), w1_ref[s, l])
        o_ref[0, s] = _compress_tail(hid, w2_ref.at[s])


def compress_prompt(z, col_block, perm, pe_n, w1bd, w2bd, *, batch, seq):
    n_blk = seq // CMP_BLOCK
    chunk = perm.shape[0]
    assert seq % chunk == 0
    return pl.pallas_call(
        functools.partial(_compress_prompt_kernel, n_blk=n_blk, chunk=chunk),
        grid=(batch,),
        in_specs=[pl.BlockSpec((seq, 2 * KVD), lambda b: (b, col_block)),
                  pl.BlockSpec(perm.shape, lambda b: (0, 0)),
                  pl.BlockSpec(pe_n.shape, lambda b: (0, 0, 0)),
                  pl.BlockSpec(w1bd.shape, lambda b: (0, 0, 0, 0), pipeline_mode=pl.Buffered(1)),
                  pl.BlockSpec(w2bd.shape, lambda b: (0, 0, 0))],
        out_specs=pl.BlockSpec((1, 2, n_blk, KVD), lambda b: (b, 0, 0, 0)),
        out_shape=jax.ShapeDtypeStruct((batch, 2, n_blk, KVD), F32),
        scratch_shapes=[pltpu.VMEM((seq // chunk, CMP_BLOCK, chunk // CMP_BLOCK, KVD), F32)],
        compiler_params=_cparams(("arbitrary",), 48),
        name="compress_prompt",
    )(z, perm, pe_n, w1bd, w2bd)


def _compress_sample_kernel(pt_ref, *refs, n_pages, page):
    del pt_ref
    page_refs = refs[:n_pages]
    perm_ref, pe_ref, w1_ref, w2_ref, o_ref, x_ref = refs[n_pages:]
    b = pl.program_id(0)
    bl = b % CMP_GROUP
    n_pairs = n_pages // 2
    blk_pair = 2 * page // CMP_BLOCK
    for pp in range(n_pairs):
        for s in range(2):
            r = slice(s * KVD, (s + 1) * KVD)
            xt = jnp.concatenate([page_refs[2 * pp][0, r, :], page_refs[2 * pp + 1][0, r, :]], axis=1)
            xt = (xt + pe_ref[s]).astype(BF16)
            xp = _dot_nt(perm_ref[...], xt)
            x_ref[s, bl * n_pairs + pp] = xp.reshape(CMP_BLOCK, blk_pair, KVD)

    @pl.when(bl == CMP_GROUP - 1)
    def _():
        rows = CMP_GROUP * n_pairs * blk_pair
        hid = [jnp.zeros((rows, KVD), F32) for _ in range(2)]
        for l in range(CMP_BLOCK):
            for s in range(2):
                xl = x_ref[s, :, l].reshape(rows, KVD)
                hid[s] = hid[s] + _dot(xl.astype(BF16), w1_ref[s, l])
        for s in range(2):
            o_ref[:, s] = _compress_tail(hid[s], w2_ref.at[s]).reshape(CMP_GROUP, n_pairs * blk_pair, KVD)


def compress_sample(cache_t, pt_flat, perm, pe_t, w1bd, w2bd, *, bs, n_pages):
    page = cache_t.shape[2]
    assert page == LANES and n_pages % 2 == 0 and bs % CMP_GROUP == 0
    n_pairs = n_pages // 2
    blk_pair = 2 * page // CMP_BLOCK
    n_blk = n_pairs * blk_pair

    def page_map(b, pt, *, p):
        return (pt[b * n_pages + p], 0, 0)

    const = lambda nd: (lambda b, pt: (0,) * nd)
    grid_spec = pltpu.PrefetchScalarGridSpec(
        num_scalar_prefetch=1,
        grid=(bs,),
        in_specs=[pl.BlockSpec((1, 2 * KVD, page), functools.partial(page_map, p=p)) for p in range(n_pages)]
        + [pl.BlockSpec(perm.shape, const(2)), pl.BlockSpec(pe_t.shape, const(3)),
           pl.BlockSpec(w1bd.shape, const(4), pipeline_mode=pl.Buffered(1)), pl.BlockSpec(w2bd.shape, const(3))],
        out_specs=pl.BlockSpec((CMP_GROUP, 2, n_blk, KVD), lambda b, pt: (b // CMP_GROUP, 0, 0, 0)),
        scratch_shapes=[pltpu.VMEM((2, CMP_GROUP * n_pairs, CMP_BLOCK, blk_pair, KVD), F32)],
    )
    return pl.pallas_call(
        functools.partial(_compress_sample_kernel, n_pages=n_pages, page=page),
        grid_spec=grid_spec,
        out_shape=jax.ShapeDtypeStruct((bs, 2, n_blk, KVD), F32),
        compiler_params=_cparams(("arbitrary",), 56),
        name="compress_sample",
    )(pt_flat, *([cache_t] * n_pages), perm, pe_t, w1bd, w2bd)


def _cmp_valid(q_pos, shape, n_cmp):
    n = lax.broadcasted_iota(jnp.int32, shape, 1) % LANES
    return (n < n_cmp) & (n * CMP_BLOCK + CMP_BLOCK - 1 <= q_pos)


def _cmp_select_prompt_kernel(q_ref, kc_ref, vc_ref, bias_ref, ocmp_ref, sel_ref, *, n_cmp, n_sb):
    qb = pl.program_id(2)
    q_pos = qb * TQ + lax.broadcasted_iota(jnp.int32, (TQ, 1), 0)
    valid = _cmp_valid(q_pos, (TQ, LANES), n_cmp)
    q = (q_ref[...] * ATTN_SCALE).astype(BF16)
    s = _dot(q, kc_ref[0, 0]) + bias_ref[0]
    p_sum = jnp.zeros((TQ, LANES), F32)
    probs = []
    for g in range(GROUP):
        e, d = _softmax_parts(s[:, g * LANES:(g + 1) * LANES], valid)
        p = e / jnp.where(d > 0, d, 1.0)
        probs.append(p.astype(BF16))
        p_sum = p_sum + p
    ocmp_ref[...] = _dot(jnp.concatenate(probs, axis=1), vc_ref[0, 0])

    k_sel = min(N_SELECT, n_sb)
    srows = SEL_PER * n_sb
    row = lax.broadcasted_iota(jnp.int32, (srows, TQ), 0)
    pos_t = qb * TQ + lax.broadcasted_iota(jnp.int32, (1, TQ), 1)
    blk = row // SEL_PER
    started = (row % SEL_PER == 0) & (blk * SEL_BLOCK <= pos_t)
    all_fit = (qb + 1) * TQ <= k_sel * SEL_BLOCK
    sel_ref[0, 0, 0] = jnp.zeros((LANES, TQ), F32)

    @pl.when(all_fit)
    def _():
        sel_ref[0, 0, 0, 0:srows, :] = jnp.where(started, 1.0, 0.0)

    @pl.when(jnp.logical_not(all_fit))
    def _():
        pt = p_sum.T
        imp = pt[0:srows]
        for k in range(1, SEL_PER):
            imp = imp + pltpu.roll(pt, LANES - k, axis=0)[0:srows]
        cur = pos_t // SEL_BLOCK
        forced = (blk == 0) | (blk == cur) | (blk == cur - 1)
        score = jnp.where(started, imp + FORCE_BONUS * forced.astype(F32), -jnp.inf)
        rank = jnp.zeros((srows, TQ), F32)
        for i in range(n_sb):
            ci = score[SEL_PER * i:SEL_PER * i + 1, :]
            rank = rank + jnp.where(row > SEL_PER * i, jnp.where(ci >= score, 1.0, 0.0), jnp.where(ci > score, 1.0, 0.0))
        sel_ref[0, 0, 0, 0:srows, :] = jnp.where((rank < k_sel) & (row % SEL_PER == 0), 1.0, 0.0)


def cmp_select_prompt(z, q_col, kc_bd, vc_bd, bias_cmp, *, batch, seq):
    n_cmp = seq // CMP_BLOCK
    n_sb = -(-seq // SEL_BLOCK)
    nq = seq // TQ
    gw = GROUP * HEAD_DIM
    assert SEL_PER * n_sb <= LANES and n_cmp <= LANES
    return pl.pallas_call(
        functools.partial(_cmp_select_prompt_kernel, n_cmp=n_cmp, n_sb=n_sb),
        grid=(batch, N_KV_HEADS, nq),
        in_specs=[pl.BlockSpec((TQ, gw), lambda b, kv, i: (b * nq + i, q_col + kv)),
                  pl.BlockSpec((1, 1, gw, GROUP * LANES), lambda b, kv, i: (b, kv, 0, 0)),
                  pl.BlockSpec((1, 1, GROUP * LANES, gw), lambda b, kv, i: (b, kv, 0, 0)),
                  pl.BlockSpec((1, TQ, GROUP * LANES), lambda b, kv, i: (kv, i, 0))],
        out_specs=[pl.BlockSpec((TQ, gw), lambda b, kv, i: (b * nq + i, kv)),
                   pl.BlockSpec((1, 1, 1, LANES, TQ), lambda b, kv, i: (b, kv, i, 0, 0))],
        out_shape=[jax.ShapeDtypeStruct((batch * seq, N_HEADS * HEAD_DIM), F32),
                   jax.ShapeDtypeStruct((batch, N_KV_HEADS, nq, LANES, TQ), F32)],
        compiler_params=_cparams(("arbitrary",) * 3, 32),
        name="cmp_select_prompt",
    )(z, kc_bd, vc_bd, bias_cmp)


def _attn_prompt_kernel(far_ref, q_ref, ks_ref, vs_ref, kw_ref, vw_ref, sel_ref, ocmp_ref, ng_ref, bt_ref, o_ref,
                        qbd_ref, ngt_ref, *state_refs):
    kv = pl.program_id(1)
    qb = pl.program_id(2)
    gw = GROUP * HEAD_DIM
    sel_state, win_state = state_refs[:5], state_refs[5:]
    blk_of_key = lax.broadcasted_iota(jnp.int32, (TQ, LANES), 0) // SEL_BLOCK
    blk_lane = lax.broadcasted_iota(jnp.int32, (TQ, LANES), 1)
    near0 = jnp.maximum(qb - 1, 0)
    eye = (lax.broadcasted_iota(jnp.int32, (gw, gw), 0) == lax.broadcasted_iota(jnp.int32, (gw, gw), 1)).astype(BF16)
    qt = _dot_nt(eye, (q_ref[...] * ATTN_SCALE).astype(BF16)).astype(BF16)
    qbd_ref[0:KVD, :] = jnp.zeros((KVD, GROUP * TQ), BF16)
    row0 = pl.multiple_of(kv * HEAD_DIM, HEAD_DIM)
    sel_neg = ((sel_ref[0, 0, 0] - 1.0) * (-NEG)).astype(BF16)
    for g in range(GROUP):
        qbd_ref[pl.ds(row0, HEAD_DIM), g * TQ:(g + 1) * TQ] = qt[g * HEAD_DIM:(g + 1) * HEAD_DIM, :]
        qbd_ref[KVD:, g * TQ:(g + 1) * TQ] = sel_neg
    vrow = lax.broadcasted_iota(jnp.int32, (V_ROWS, KVD), 0)
    vpick = ((lax.broadcasted_iota(jnp.int32, (V_ROWS, KVD), 1) == kv * HEAD_DIM + vrow) & (vrow < HEAD_DIM))
    vpick = vpick.astype(BF16)
    ones_row = lax.broadcasted_iota(jnp.int32, (V_ROWS, TQ), 0) == HEAD_DIM

    def chunk(state, j, k_ref, v_ref, selected, tile):
        s_ref, p_ref, m_ref, a_ref, acc_ref = state
        start = pl.multiple_of(j * TQ, TQ)
        k = k_ref[pl.ds(start, TQ), :].astype(BF16)
        if selected:
            onehot = (blk_lane == SEL_PER * (blk_of_key + j * (TQ // SEL_BLOCK))).astype(BF16)
            s_ref[...] = _dot(jnp.concatenate([k, onehot], axis=1), qbd_ref[...])
        else:
            s_ref[...] = _dot(k, qbd_ref[0:KVD, :])
        vt = jnp.where(ones_row, 1.0, _dot_nt(vpick, v_ref[pl.ds(start, TQ), :].astype(BF16))).astype(BF16)
        for g in range(GROUP):
            far = far_ref[kv * GROUP + g]
            for half in range(TQ // SM_LANES):
                c = slice(g * TQ + half * SM_LANES, g * TQ + (half + 1) * SM_LANES)
                u = s_ref[:, c]
                if tile is not None:
                    u = u + bt_ref[g, tile, :, half * SM_LANES:(half + 1) * SM_LANES]
                m_old = m_ref[:, c]
                mx = jnp.max(u, axis=0, keepdims=True)
                m_new = jnp.maximum(m_old, mx + far if tile is None else mx)
                a_ref[:, c] = jnp.exp(m_old - m_new)
                m_ref[:, c] = m_new
                p_ref[:, c] = jnp.exp(u - (m_new - far if tile is None else m_new)).astype(BF16)
        acc_ref[...] = a_ref[...] * acc_ref[...] + _dot(vt, p_ref[...])

    def finish(state):
        acc_ref = state[4]
        l = acc_ref[HEAD_DIM:HEAD_DIM + 1, :]
        return acc_ref[0:HEAD_DIM, :] / jnp.where(l > 0, l, 1.0)

    for state in (sel_state, win_state):
        state[2][...] = jnp.full(state[2].shape, NEG, F32)
        state[4][...] = jnp.zeros(state[4].shape, F32)

    def far_body(j, c):
        chunk(sel_state, j, ks_ref, vs_ref, True, None)
        return c
    lax.fori_loop(0, near0, far_body, 0)

    @pl.when(qb >= WINDOW // TQ)
    def _():
        chunk(win_state, qb - WINDOW // TQ, kw_ref, vw_ref, False, 0)

    def near_body(j, c):
        tile = j - (qb - 1) + 1
        chunk(sel_state, j, ks_ref, vs_ref, True, tile)
        chunk(win_state, j, kw_ref, vw_ref, False, tile)
        return c
    lax.fori_loop(near0, qb + 1, near_body, 0)
    o_sel = finish(sel_state)
    o_win = finish(win_state)
    ngt_ref[...] = ng_ref[...].T
    ocmp_t = ocmp_ref[...].T
    outs = []
    for g in range(GROUP):
        gate = lambda r: ngt_ref[pl.ds(r * N_HEADS + kv * GROUP + g, 1), :]
        c = slice(g * TQ, (g + 1) * TQ)
        outs.append(gate(0) * ocmp_t[g * HEAD_DIM:(g + 1) * HEAD_DIM, :] + gate(1) * o_sel[:, c]
                    + gate(2) * o_win[:, c])
    o_t = jnp.concatenate(outs, axis=0).astype(BF16)
    o_ref[...] = _dot_nt(eye, o_t).astype(o_ref.dtype)


def attn_prompt(far, z, q_col, kv_col, sel, ocmp, ng, bias_tile, *, batch, seq):
    assert WINDOW == 2 * TQ and seq % TQ == 0 and TQ % SM_LANES == 0 and TQ == GROUP * HEAD_DIM == KVD
    cols = GROUP * TQ
    nq = seq // TQ
    kv_spec = lambda slot: pl.BlockSpec((seq, KVD), lambda b, kv, i: (b, kv_col + slot))
    tok_spec = lambda col0: pl.BlockSpec((TQ, KVD), lambda b, kv, i: (b * nq + i, col0 + kv))
    return pl.pallas_call(
        _attn_prompt_kernel,
        grid=(batch, N_KV_HEADS, nq),
        in_specs=[pl.BlockSpec(memory_space=pltpu.SMEM),
                  tok_spec(q_col), kv_spec(0), kv_spec(1), kv_spec(2), kv_spec(3),
                  pl.BlockSpec((1, 1, 1, LANES, TQ), lambda b, kv, i: (b, kv, i, 0, 0)),
                  tok_spec(0),
                  pl.BlockSpec((TQ, LANES), lambda b, kv, i: (b * nq + i, 0)),
                  pl.BlockSpec((GROUP, 3, TQ, TQ), lambda b, kv, i: (kv, 0, 0, 0))],
        out_specs=tok_spec(0),
        out_shape=jax.ShapeDtypeStruct((batch * seq, N_HEADS * HEAD_DIM), BF16),
        scratch_shapes=[pltpu.VMEM((KVD + LANES, cols), BF16), pltpu.VMEM((LANES, TQ), F32)]
        + 2 * [pltpu.VMEM((TQ, cols), F32), pltpu.VMEM((TQ, cols), BF16),
               pltpu.VMEM((1, cols), F32), pltpu.VMEM((1, cols), F32), pltpu.VMEM((V_ROWS, cols), F32)],
        compiler_params=_cparams(("arbitrary",) * 3, 48),
        name="attn_prompt",
    )(far, z, z, z, z, z, sel, ocmp, ng, bias_tile)


def _diag_heads(x, ts):
    rows = x.shape[0]
    kv_of_row = (lax.broadcasted_iota(jnp.int32, (rows, HEAD_DIM), 0) % (N_KV_HEADS * ts)) // ts
    out = jnp.zeros((rows, HEAD_DIM), F32)
    for kv in range(N_KV_HEADS):
        out = out + jnp.where(kv_of_row == kv, x[:, kv * HEAD_DIM:(kv + 1) * HEAD_DIM], 0.0)
    return out


def _attn_sample_kernel(pt_ref, *refs, n_pages, page, ts, past_len, wb):
    del pt_ref
    page_refs = refs[:SEQ_PER_STEP * n_pages]
    (q_ref, knew_ref, win_ref, wnew_ref, kc_ref, vc_ref, bcmp_ref, bsel_ref, bseln_ref, bwin_ref, bwinn_ref,
     expand_ref, ng_ref, o_ref, kt_ref, vt_ref) = refs[SEQ_PER_STEP * n_pages:]
    rows = GROUP * N_KV_HEADS * ts
    kvt = N_KV_HEADS * ts
    n_cmp = past_len // CMP_BLOCK
    n_sb = -(-(past_len + ts) // SEL_BLOCK)
    t_of_row = lax.broadcasted_iota(jnp.int32, (rows, 1), 0) % ts
    q_pos = past_len + t_of_row
    new_col = lax.broadcasted_iota(jnp.int32, (rows, 16), 1)
    new_ok = (new_col <= t_of_row) & (new_col < ts)
    new_lane = SEL_PER * (past_len // SEL_BLOCK)
    win_dist = wb + t_of_row - lax.broadcasted_iota(jnp.int32, (rows, wb), 1)
    win_mask = (win_dist >= 0) & (win_dist <= WINDOW)

    for k in range(SEQ_PER_STEP):
        q = q_ref[k]
        s = _dot_nt(q, kc_ref[k]) + bcmp_ref[...]
        e, d = _softmax_parts(s, _cmp_valid(q_pos, (rows, LANES), n_cmp))
        p = e / jnp.where(d > 0, d, 1.0)
        o_cmp = _diag_heads(_dot(p.astype(BF16), vc_ref[k]), ts)
        p_sum = p[0:kvt]
        for g in range(1, GROUP):
            p_sum = p_sum + p[g * kvt:(g + 1) * kvt]
        sel = _topk_mask(_block_scores(p_sum, q_pos[0:kvt], n_sb), n_sb, min(N_SELECT, n_sb))
        sel = jnp.concatenate([sel] * GROUP, axis=0)

        for pg in range(n_pages):
            pref = page_refs[k * n_pages + pg]
            kt_ref[k, :, pg * page:(pg + 1) * page] = pref[0, 0:KVD, :].astype(BF16)
            vt_ref[k, :, pg * page:(pg + 1) * page] = pref[0, KVD:2 * KVD, :].astype(BF16)
        s = _dot(q, kt_ref[k]) + bsel_ref[...]
        mask = _dot(sel.astype(BF16), expand_ref[...]) > 0.5
        sn = _dot_nt(q, knew_ref[k, :, 0:KVD]) + bseln_ref[...]
        mask_n = new_ok & (sel[:, new_lane:new_lane + 1] > 0.5)
        o_sel = _two_part_attention(s, mask, vt_ref[k], sn, mask_n, knew_ref[k, :, KVD:2 * KVD], ts)

        s = _dot(q, win_ref[k, 0:KVD, :].astype(BF16)) + bwin_ref[...]
        sn = _dot_nt(q, wnew_ref[k, :, 0:KVD]) + bwinn_ref[...]
        o_win = _two_part_attention(s, win_mask, win_ref[k, KVD:2 * KVD, :].astype(BF16), sn, new_ok,
                                    wnew_ref[k, :, KVD:2 * KVD], ts)

        ng = ng_ref[k]
        o_ref[k] = (ng[:, 0:1] * o_cmp + ng[:, 1:2] * o_sel + ng[:, 2:3] * o_win).astype(o_ref.dtype)


def _two_part_attention(s, mask, vt, sn, mask_n, vn, ts):
    s = jnp.where(mask, s, NEG)
    sn = jnp.where(mask_n, sn, NEG)
    m = jnp.maximum(jnp.max(s, axis=-1, keepdims=True), jnp.max(sn, axis=-1, keepdims=True))
    e = jnp.where(mask, jnp.exp(s - m), 0.0)
    en = jnp.where(mask_n, jnp.exp(sn - m), 0.0)
    d = jnp.sum(e, axis=-1, keepdims=True) + jnp.sum(en, axis=-1, keepdims=True)
    inv = jnp.where(d > 0, d, 1.0)
    o = _dot_nt((e / inv).astype(BF16), vt) + _dot((en / inv).astype(BF16), vn)
    return _diag_heads(o, ts)


def attn_sample(cache_t, pt_flat, q_bd, knew, win_t, wnew, kc_s, vc_s, bcmp, bsel, bseln, bwin, bwinn, expand, ng_s,
                *, bs, ts, n_pages, past_len):
    page = cache_t.shape[2]
    wb = win_t.shape[2]
    rows = GROUP * N_KV_HEADS * ts
    assert past_len == n_pages * page and past_len % SEL_BLOCK == 0 and ts <= 16
    assert (past_len + ts) // CMP_BLOCK == past_len // CMP_BLOCK
    sps = SEQ_PER_STEP
    assert bs % sps == 0

    def page_map(b, pt, *, p):
        return (pt[b * sps * n_pages + p], 1, 0)

    per_b = lambda shape: pl.BlockSpec((sps,) + shape, lambda b, pt: (b,) + (0,) * len(shape))
    const = lambda a: pl.BlockSpec(a.shape, lambda b, pt: (0,) * a.ndim)
    grid_spec = pltpu.PrefetchScalarGridSpec(
        num_scalar_prefetch=1,
        grid=(bs // sps,),
        in_specs=[pl.BlockSpec((1, 2 * KVD, page), functools.partial(page_map, p=p)) for p in range(sps * n_pages)]
        + [per_b((rows, KVD)), per_b((16, 2 * KVD)), per_b((2 * KVD, wb)), per_b((16, 2 * KVD)),
           per_b((LANES, KVD)), per_b((LANES, KVD)),
           const(bcmp), const(bsel), const(bseln), const(bwin), const(bwinn), const(expand),
           per_b((rows, 3))],
        out_specs=per_b((rows, HEAD_DIM)),
        scratch_shapes=[pltpu.VMEM((sps, KVD, past_len), BF16), pltpu.VMEM((sps, KVD, past_len), BF16)],
    )
    return pl.pallas_call(
        functools.partial(_attn_sample_kernel, n_pages=n_pages, page=page, ts=ts, past_len=past_len, wb=wb),
        grid_spec=grid_spec,
        out_shape=jax.ShapeDtypeStruct((bs, rows, HEAD_DIM), BF16),
        compiler_params=_cparams(("arbitrary",), 48),
        name="attn_sample",
    )(pt_flat, *([cache_t] * (sps * n_pages)), q_bd, knew, win_t, wnew, kc_s, vc_s, bcmp, bsel, bseln, bwin, bwinn,
      expand, ng_s)


def _merge_kernel(a_ref, o_ref, g0_ref, g1_ref, h_ref, wp_ref, wn_ref, wo_ref, gp_ref, out_ref):
    merged = g0_ref[...] * _dot(a_ref[...], wp_ref[...]) + g1_ref[...] * _dot(o_ref[...], wn_ref[...])
    r = _dot(merged.astype(BF16), wo_ref[...])
    out_ref[...] = h_ref[...] + _rms(r, gp_ref[...])


def merge(a, o, gates, h, wp, wn, wo, gp):
    n, d = h.shape
    tm = min(256, n)
    pw = a.shape[1]
    qw = o.shape[1]
    once = pl.Buffered(1)
    return pl.pallas_call(
        _merge_kernel,
        grid=(n // tm,),
        in_specs=[pl.BlockSpec((tm, pw), lambda i: (i, 0)),
                  pl.BlockSpec((tm, qw), lambda i: (i, 0)),
                  pl.BlockSpec((tm, d), lambda i: (i, 0)),
                  pl.BlockSpec((tm, d), lambda i: (i, 1)),
                  pl.BlockSpec((tm, d), lambda i: (i, 0)),
                  pl.BlockSpec(wp.shape, lambda i: (0, 0), pipeline_mode=once),
                  pl.BlockSpec(wn.shape, lambda i: (0, 0), pipeline_mode=once),
                  pl.BlockSpec(wo.shape, lambda i: (0, 0), pipeline_mode=once),
                  pl.BlockSpec((1, d), lambda i: (0, 0))],
        out_specs=pl.BlockSpec((tm, d), lambda i: (i, 0)),
        out_shape=jax.ShapeDtypeStruct((n, d), F32),
        compiler_params=_cparams(("arbitrary",), 56),
        name="merge",
    )(a, o, gates, gates, h, wp, wn, wo, gp)


def _ffn_kernel(h_ref, halo_ref, gpre_ref, wv_ref, wg_ref, cw_ref, cb_ref, wd_ref, gpost_ref,
                out_ref, tail_ref, xn_ref, acc_ref, *, tm, halo, shift, tiles_per_seq, tail, from_rows):
    i = pl.program_id(0)
    f = pl.program_id(1)

    @pl.when(f == 0)
    def _():
        xn_ref[halo:, :] = _rms(h_ref[...], gpre_ref[...]).astype(BF16)
        if from_rows:
            xn_ref[0:halo, :] = _rms(halo_ref[...], gpre_ref[...]).astype(BF16)
        acc_ref[...] = jnp.zeros_like(acc_ref)

    tf = wv_ref.shape[1]
    parts = [slice(k * tf // FF_PARTS, (k + 1) * tf // FF_PARTS) for k in range(FF_PARTS)]
    ups = []
    for c in parts:
        val = _dot(xn_ref[halo:, :], wv_ref[:, c])
        if from_rows:
            gcat = _dot(xn_ref[...], wg_ref[:, c])
            is_hist = lax.broadcasted_iota(jnp.int32, gcat.shape, 0) < halo
            gcat = jnp.where(is_hist & (i % tiles_per_seq == 0), 0.0, gcat)
        else:
            gcat = jnp.concatenate([halo_ref[:, c], _dot(xn_ref[halo:, :], wg_ref[:, c])], axis=0)
        ups.append((val, gcat))
    down = None
    for c, (val, gcat) in zip(parts, ups):
        conv = cb_ref[:, c]
        for j in range(CONV_WIDTH):
            back = (CONV_WIDTH - 1 - j) * shift
            if back % 8 == 0:
                tap = gcat[halo - back:halo - back + tm]
            else:
                tap = pltpu.roll(gcat, back, axis=0)[halo:]
            conv = conv + cw_ref[j:j + 1, c] * tap
        act = jax.nn.gelu(conv, approximate=True) * val
        part = _dot(act.astype(BF16), wd_ref[c, :])
        down = part if down is None else down + part
        tail_ref[:, c] = gcat[halo + tm - tail:, :]
    acc_ref[...] += down

    @pl.when(f == pl.num_programs(1) - 1)
    def _():
        out_ref[...] = h_ref[...] + _rms(acc_ref[...], gpost_ref[...])


def ffn(h, halo_src, gpre, w_up, cw, cb, wd, gpost, *, tm, tf, halo, shift, tiles_per_seq, tail):
    n, d = h.shape
    dff = wd.shape[0]
    nt = n // tm
    assert dff % tf == 0
    from_rows = halo_src is None
    if from_rows:
        halo_src = h
        halo_spec = pl.BlockSpec((halo, d), lambda i, f: (jnp.maximum(i * (tm // halo) - 1, 0), 0))
    else:
        assert nt == 1
        halo_spec = pl.BlockSpec((halo, tf), lambda i, f: (0, f))
    return pl.pallas_call(
        functools.partial(_ffn_kernel, tm=tm, halo=halo, shift=shift, tiles_per_seq=tiles_per_seq, tail=tail,
                          from_rows=from_rows),
        grid=(nt, dff // tf),
        in_specs=[pl.BlockSpec((tm, d), lambda i, f: (i, 0)),
                  halo_spec,
                  pl.BlockSpec((1, d), lambda i, f: (0, 0)),
                  pl.BlockSpec((d, tf), lambda i, f: (0, f)),
                  pl.BlockSpec((d, tf), lambda i, f: (0, dff // tf + f)),
                  pl.BlockSpec((CONV_WIDTH, tf), lambda i, f: (0, f)),
                  pl.BlockSpec((1, tf), lambda i, f: (0, f)),
                  pl.BlockSpec((tf, d), lambda i, f: (f, 0)),
                  pl.BlockSpec((1, d), lambda i, f: (0, 0))],
        out_specs=[pl.BlockSpec((tm, d), lambda i, f: (i, 0)),
                   pl.BlockSpec((tail, tf), lambda i, f: (i, f))],
        out_shape=[jax.ShapeDtypeStruct((n, d), F32),
                   jax.ShapeDtypeStruct((nt * tail, dff), F32)],
        scratch_shapes=[pltpu.VMEM((tm + halo, d), BF16), pltpu.VMEM((tm, d), F32)],
        compiler_params=_cparams(("arbitrary", "arbitrary"), 56),
        name="ffn",
    )(h, halo_src, gpre, w_up, w_up, cw, cb, wd, gpost)


def _block_diag4(w):
    eye = jnp.eye(N_KV_HEADS, dtype=w.dtype)
    out = jnp.einsum("gh,...ab->...gahb", eye, w)
    return out.reshape(w.shape[:-2] + (N_KV_HEADS * w.shape[-2], N_KV_HEADS * w.shape[-1]))


def _cmp_dist(q_pos):
    return q_pos[:, None] - (jnp.arange(LANES) * CMP_BLOCK + CMP_BLOCK - 1)[None, :]


def _layer(hp, hs_t, cache_l, pt_flat, win_l, pool_l, conv_l, wts, table, dims):
    (g_pre_mix, w_in, pe_k, w1_k, w2_k, pe_v, w1_v, w2_v, w_pool_grp, pool_scale, w_pool_proj, w_nsa_proj,
     w_out, g_post_mix, g_pre_ffn, w_up, conv_w, conv_b, w_down, g_post_ffn) = wts
    batch, seq, bs, ts, n_pages, page = dims
    d_model = hp.shape[1]
    pool_w = w_pool_grp.shape[0] * w_pool_grp.shape[1]
    q_w = N_HEADS * HEAD_DIM
    d_ff = w_down.shape[0]
    past_len = n_pages * page
    c_q, c_kv, c_ng, c_mg = pool_w, pool_w + q_w, pool_w + q_w + 6 * KVD, pool_w + q_w + 6 * KVD + 3 * N_HEADS
    assert pool_w == q_w == 4 * KVD, "column blocks below assume equal widths"

    row = lambda v: v.reshape(1, -1)
    w_a = w_in[:, :c_ng].astype(BF16)
    w_ng = jnp.pad(w_in[:, c_ng:c_mg], ((0, 0), (0, LANES - 3 * N_HEADS))).astype(BF16)
    w_mg = w_in[:, c_mg:].astype(BF16)
    w_grp = w_pool_grp.astype(BF16)
    w1bd = jnp.stack([_block_diag4(w1_k), _block_diag4(w1_v)]).astype(BF16)
    w2bd = jnp.stack([_block_diag4(w2_k), _block_diag4(w2_v)]).astype(BF16)
    chunk = 2 * page
    blk_chunk = chunk // CMP_BLOCK
    regroup = (jnp.arange(chunk) % blk_chunk) * CMP_BLOCK + jnp.arange(chunk) // blk_chunk
    perm = (jnp.arange(chunk)[None, :] == regroup[:, None]).astype(BF16)
    pe_n = jnp.stack([jnp.tile(pe, (blk_chunk, N_KV_HEADS)) for pe in (pe_k, pe_v)])
    pe_t = pe_n.transpose(0, 2, 1)
    w_up16 = w_up.astype(BF16)
    wd = w_down.astype(BF16)
    wpp, wnp, wo = w_pool_proj.astype(BF16), w_nsa_proj.astype(BF16), w_out.astype(BF16)
    heads = table.reshape(N_BUCKETS, N_HEADS)

    z = rms_matmul(hp, row(g_pre_mix), w_a, tm=1024, tn=512, name="proj_main_prompt")
    gates = rms_matmul(hp, row(g_pre_mix), w_mg, tm=1024, tn=512, act="sigmoid", name="proj_gate_prompt")
    ng = rms_matmul(hp, row(g_pre_mix), w_ng, tm=1024, tn=LANES, act="sigmoid", name="proj_ng_prompt")
    a_p = pool_prompt(z, w_grp, row(pool_scale), batch=batch, seq=seq, tt=512)
    z3 = z.reshape(batch, seq, -1)
    kv_rows_p = z3[:, :, c_kv:c_kv + 4 * KVD].reshape(batch, seq, N_PAGED_SLOTS, N_KV_HEADS, HEAD_DIM)
    win_p = z3[:, seq - min(WINDOW, seq):, c_kv + 4 * KVD:c_ng].reshape(batch, -1, 2, N_KV_HEADS, HEAD_DIM)
    pool_new_p = z3[:, seq - POOL_HIST:, :pool_w]

    cmp_p = compress_prompt(z, c_kv // (2 * KVD), perm, pe_n, w1bd, w2bd, batch=batch, seq=seq)
    cmp_p = jnp.pad(cmp_p, ((0, 0), (0, 0), (0, LANES - cmp_p.shape[2]), (0, 0)))
    cmp_p = cmp_p.reshape(batch, 2, LANES, N_KV_HEADS, HEAD_DIM).astype(BF16)
    eye_g = jnp.eye(GROUP, dtype=BF16)
    kc_bd = jnp.einsum("bnkd,gh->bkgdhn", cmp_p[:, 0], eye_g).reshape(batch, N_KV_HEADS, KVD, GROUP * LANES)
    vc_bd = jnp.einsum("bnkd,gh->bkgnhd", cmp_p[:, 1], eye_g).reshape(batch, N_KV_HEADS, GROUP * LANES, KVD)
    pos_p = jnp.arange(seq)
    bias_cmp = _bias_lookup(heads.T, _cmp_dist(pos_p))
    bias_cmp = (bias_cmp.reshape(N_KV_HEADS, GROUP, seq, LANES).transpose(0, 2, 1, 3)
                .reshape(N_KV_HEADS, seq, GROUP * LANES))
    ocmp, sel = cmp_select_prompt(z, c_q // KVD, kc_bd, vc_bd, bias_cmp, batch=batch, seq=seq)
    tok_key = jnp.arange(TQ)[None, :] - jnp.arange(TQ)[:, None]
    near = _bias_lookup(heads.T, jnp.stack([TQ + tok_key, tok_key]))
    far_edge = jnp.where(tok_key <= 0, heads[N_BUCKETS - 1][:, None, None], NEG)
    bias_tile = jnp.stack([far_edge, near[:, 0], jnp.where(tok_key >= 0, near[:, 1], NEG)], axis=1)
    o_p = attn_prompt(heads[N_BUCKETS - 1], z, c_q // KVD, c_kv // KVD + 2, sel, ocmp, ng, bias_tile,
                      batch=batch, seq=seq)
    hp = merge(a_p, o_p, gates, hp, wpp, wnp, wo, row(g_post_mix))
    hp, tails = ffn(hp, None, row(g_pre_ffn), w_up16, conv_w, row(conv_b), wd, row(g_post_ffn),
                    tm=512, tf=512, halo=16, shift=1, tiles_per_seq=seq // 512, tail=8)
    conv_new_p = tails.reshape(batch, seq // 512, 8, d_ff)[:, -1, 8 - (CONV_WIDTH - 1):]

    n_s = ts * bs
    zs = rms_matmul(hs_t, row(g_pre_mix), w_a, tm=n_s, tn=512, name="proj_main_sample")
    gates_s = rms_matmul(hs_t, row(g_pre_mix), w_mg, tm=n_s, tn=512, act="sigmoid", name="proj_gate_sample")
    ng_s = rms_matmul(hs_t, row(g_pre_mix), w_ng, tm=n_s, tn=LANES, act="sigmoid", name="proj_ng_sample")
    hist_t = pool_l.transpose(1, 0, 2)
    a_s = pool_sample(zs, hist_t, w_grp, row(pool_scale), ts=ts, bs=bs, past_len=past_len)
    zs3 = zs.reshape(ts, bs, -1)
    kv_rows_s = zs3[:, :, c_kv:c_kv + 4 * KVD].transpose(1, 0, 2).reshape(bs, ts, N_PAGED_SLOTS, N_KV_HEADS, HEAD_DIM)
    win_rows = zs3[:, :, c_kv + 4 * KVD:c_ng].transpose(1, 0, 2).reshape(bs, ts, 2, N_KV_HEADS, HEAD_DIM)
    win_new_s = jnp.concatenate([win_l, win_rows], axis=1)[:, -win_l.shape[1]:]
    pool_new_s = jnp.concatenate([pool_l, zs3[:, :, :pool_w].transpose(1, 0, 2)], axis=1)[:, -POOL_HIST:]

    cache_t = cache_l.transpose(0, 2, 3, 4, 1).reshape(cache_l.shape[0], N_PAGED_SLOTS * KVD, page)
    cmp_s = compress_sample(cache_t, pt_flat, perm, pe_t, w1bd, w2bd, bs=bs, n_pages=n_pages)
    cmp_s = jnp.pad(cmp_s, ((0, 0), (0, 0), (0, LANES - cmp_s.shape[2]), (0, 0))).astype(BF16)

    rows = GROUP * N_KV_HEADS * ts
    q_s = zs3[:, :, c_q:c_kv].reshape(ts, bs, N_KV_HEADS, GROUP, HEAD_DIM).transpose(1, 3, 2, 0, 4)
    q_bd = jnp.einsum("bgktd,kj->bgktjd", q_s * ATTN_SCALE, jnp.eye(N_KV_HEADS, dtype=F32))
    q_bd = q_bd.reshape(bs, rows, KVD).astype(BF16)
    pad_rows = lambda x: jnp.pad(x.transpose(1, 0, 2), ((0, 0), (0, 16 - ts), (0, 0))).astype(BF16)
    knew = pad_rows(zs3[:, :, c_kv + 2 * KVD:c_kv + 4 * KVD])
    wnew = pad_rows(zs3[:, :, c_kv + 4 * KVD:c_ng])
    wb = win_l.shape[1]
    win_t = win_l.transpose(0, 2, 3, 4, 1).reshape(bs, 2 * KVD, wb)
    r_idx = jnp.arange(rows)
    r_head = ((r_idx % (N_KV_HEADS * ts)) // ts) * GROUP + r_idx // (N_KV_HEADS * ts)
    r_pos = past_len + r_idx % ts
    row_tab = heads.T[r_head]
    rbias = lambda dist: _bias_lookup_rows(row_tab, dist)
    bcmp = rbias(_cmp_dist(r_pos))
    bsel = rbias(r_pos[:, None] - jnp.arange(past_len)[None, :])
    bnew = rbias((r_idx % ts)[:, None] - jnp.arange(16)[None, :])
    bwin = rbias(r_pos[:, None] - (past_len - wb + jnp.arange(wb))[None, :])
    expand = (jnp.arange(LANES)[:, None] == SEL_PER * (jnp.arange(past_len) // SEL_BLOCK)[None, :]).astype(BF16)
    ng_rows = ng_s[:, :3 * N_HEADS].reshape(ts, bs, 3, N_KV_HEADS, GROUP).transpose(1, 4, 3, 0, 2).reshape(bs, rows, 3)
    o_s = attn_sample(cache_t, pt_flat, q_bd, knew, win_t, wnew, cmp_s[:, 0], cmp_s[:, 1], bcmp, bsel, bnew, bwin,
                      bnew, expand, ng_rows, bs=bs, ts=ts, n_pages=n_pages, past_len=past_len)
    o_s = o_s.reshape(bs, GROUP, N_KV_HEADS, ts, HEAD_DIM).transpose(3, 0, 2, 1, 4).reshape(n_s, q_w)
    hs_t = merge(a_s, o_s, gates_s, hs_t, wpp, wnp, wo, row(g_post_mix))
    conv_hist = conv_l.transpose(1, 0, 2).reshape((CONV_WIDTH - 1) * bs, d_ff)
    hs_t, tails_s = ffn(hs_t, conv_hist, row(g_pre_ffn), w_up16, conv_w, row(conv_b), wd, row(g_post_ffn),
                        tm=n_s, tf=512, halo=(CONV_WIDTH - 1) * bs, shift=bs, tiles_per_seq=1,
                        tail=(CONV_WIDTH - 1) * bs)
    conv_new_s = tails_s.reshape(CONV_WIDTH - 1, bs, d_ff).transpose(1, 0, 2)
    return hp, hs_t, (kv_rows_p, kv_rows_s, win_p, win_new_s, pool_new_p, pool_new_s, conv_new_p, conv_new_s)


def kernel(x_prompt, x_sample, cache_kv, page_table, state_kv_win, state_pool, state_conv, g_pre_mix, w_in, pe_cmp_k, w1_cmp_k, w2_cmp_k, pe_cmp_v, w1_cmp_v, w2_cmp_v, rel_bias, w_pool_grp, pool_scale, w_pool_proj, w_nsa_proj, w_out, g_post_mix, g_pre_ffn, w_up, conv_w, conv_b, w_down, g_post_ffn):
    batch, seq, d_model = x_prompt.shape
    bs, ts, _ = x_sample.shape
    depth = cache_kv.shape[0]
    n_pages, page = page_table.shape[1], cache_kv.shape[2]
    assert ts >= CONV_WIDTH - 1
    dims = (batch, seq, bs, ts, n_pages, page)
    pt_flat = page_table.reshape(-1).astype(jnp.int32)
    hp = x_prompt.reshape(batch * seq, d_model)
    hs_t = x_sample.transpose(1, 0, 2).reshape(ts * bs, d_model)
    outs = [[] for _ in range(8)]
    per_layer = (g_pre_mix, w_in, pe_cmp_k, w1_cmp_k, w2_cmp_k, pe_cmp_v, w1_cmp_v, w2_cmp_v, w_pool_grp, pool_scale,
                 w_pool_proj, w_nsa_proj, w_out, g_post_mix, g_pre_ffn, w_up, conv_w, conv_b, w_down, g_post_ffn)
    for l in range(depth):
        wts = tuple(w[l] for w in per_layer)
        hp, hs_t, states = _layer(hp, hs_t, cache_kv[l], pt_flat, state_kv_win[l], state_pool[l], state_conv[l],
                                  wts, rel_bias, dims)
        for acc, s in zip(outs, states):
            acc.append(s)
    y_prompt = hp.reshape(batch, seq, d_model)
    y_sample = hs_t.reshape(ts, bs, d_model).transpose(1, 0, 2)
    return (y_prompt, y_sample) + tuple(jnp.stack(o) for o in outs)
```
